```python
import math
import jax, jax.numpy as jnp
from jax import lax
import numpy as np

D_MODEL = 2048
BATCH = 1
SEQ = 8192
DEPTH = 1

HEAD_DIM = 128
MIX_WIDTH = D_MODEL
FOURIER_WIDTH = MIX_WIDTH // 4
ATTN_WIDTH = MIX_WIDTH - FOURIER_WIDTH
N_Q_HEADS = ATTN_WIDTH // HEAD_DIM
GQA_GROUP = 3
N_KV_HEADS = N_Q_HEADS // GQA_GROUP
KV_WIDTH = N_KV_HEADS * HEAD_DIM
FOURIER_GROUP_DIM = 128
N_FOURIER_GROUPS = FOURIER_WIDTH // FOURIER_GROUP_DIM
IN_WIDTH = ATTN_WIDTH + 2 * KV_WIDTH + FOURIER_WIDTH
WINDOW = 128
BLOCK = 128
D_FF = ((8 * D_MODEL // 3 + 127) // 128) * 128
CONV_WIDTH = 3
EPS = 1e-6
NEG_INF = -1e30

kernel_name = "hybrid_window_gqa_fnet_convffn_block"


def _rmsnorm(x, g):
    xf = x.astype(jnp.float32)
    y = xf * lax.rsqrt(jnp.mean(xf * xf, axis=-1, keepdims=True) + EPS)
    return (y * g.astype(jnp.float32)).astype(x.dtype)


def _alibi_slopes(n_heads):
    def pow2_slopes(n):
        start = 2.0 ** (-8.0 / n)
        return [start ** (i + 1) for i in range(n)]
    if math.log2(n_heads).is_integer():
        s = pow2_slopes(n_heads)
    else:
        closest = 2 ** int(math.floor(math.log2(n_heads)))
        s = pow2_slopes(closest) + pow2_slopes(2 * closest)[0::2][: n_heads - closest]
    return jnp.asarray(np.array(s, dtype=np.float32))


def _window_attention(q, k, v, sink):
    B, S = q.shape[0], q.shape[1]
    nb = S // BLOCK
    qb = q.reshape(B, nb, BLOCK, N_KV_HEADS, GQA_GROUP, HEAD_DIM)

    def band(t):
        tb = t.reshape(B, nb, BLOCK, N_KV_HEADS, HEAD_DIM)
        tp = jnp.pad(tb, ((0, 0), (1, 1), (0, 0), (0, 0), (0, 0)))
        return jnp.concatenate([tp[:, :-2], tp[:, 1:-1], tp[:, 2:]], axis=2)

    kb, vb = band(k), band(v)
    scores = jnp.einsum('bnqkgd,bnskd->bnkgqs', qb, kb).astype(jnp.float32)
    scores = scores * (HEAD_DIM ** -0.5)

    qi = jnp.arange(BLOCK)[:, None]
    kj = jnp.arange(3 * BLOCK)[None, :]
    rel = kj - BLOCK - qi
    s_abs = jnp.arange(nb)[:, None, None] * BLOCK - BLOCK + kj[None]
    valid = (jnp.abs(rel) <= WINDOW)[None] & (s_abs >= 0) & (s_abs < S)

    slopes = _alibi_slopes(N_Q_HEADS).reshape(N_KV_HEADS, GQA_GROUP)
    alibi = -slopes[:, :, None, None] * jnp.abs(rel).astype(jnp.float32)[None, None]
    scores = scores + alibi[None, None]
    scores = jnp.where(valid[None, :, None, None], scores, NEG_INF)

    sink_b = sink.astype(jnp.float32).reshape(1, 1, N_KV_HEADS, GQA_GROUP, 1, 1)
    m = jnp.maximum(jnp.max(scores, axis=-1, keepdims=True), sink_b)
    p = jnp.exp(scores - m)
    p = p / (jnp.sum(p, axis=-1, keepdims=True) + jnp.exp(sink_b - m))
    out = jnp.einsum('bnkgqs,bnskd->bnqkgd', p.astype(v.dtype), vb)
    return out.reshape(B, S, N_Q_HEADS * HEAD_DIM)


def _fourier_mix(u, w_fourier):
    B, S = u.shape[0], u.shape[1]
    ug = u.reshape(B, S, N_FOURIER_GROUPS, FOURIER_GROUP_DIM).astype(jnp.float32)
    f = jnp.real(jnp.fft.fft2(ug, axes=(1, 3), norm='ortho'))
    y = jnp.einsum('bsgc,gcd->bsgd', f.astype(u.dtype), w_fourier)
    return y.reshape(B, S, FOURIER_WIDTH)


def _conv_ffn(h, w_up, dw_w, dw_b, w_down):
    up = h @ w_up
    gate, val = up[..., :D_FF], up[..., D_FF:]
    gate = lax.conv_general_dilated(
        gate, dw_w, window_strides=(1,), padding=((CONV_WIDTH // 2, CONV_WIDTH // 2),),
        dimension_numbers=('NWC', 'WIO', 'NWC'), feature_group_count=D_FF) + dw_b
    act = jax.nn.gelu(gate, approximate=False) * val
    return act @ w_down


def setup_inputs(seed: int = 0) -> dict:
    key = jax.random.key(seed)
    ks = jax.random.split(key, 16)
    f32 = jnp.float32
    x = jax.random.normal(ks[0], (BATCH, SEQ, D_MODEL), f32)
    norm1_g = 1.0 + 0.02 * jax.random.normal(ks[1], (D_MODEL,), f32)
    w_in = jax.random.normal(ks[2], (D_MODEL, IN_WIDTH), f32) * D_MODEL ** -0.5
    sink = 0.5 * jax.random.normal(ks[3], (N_Q_HEADS,), f32)
    w_fourier = jax.random.normal(ks[4], (N_FOURIER_GROUPS, FOURIER_GROUP_DIM, FOURIER_GROUP_DIM), f32) * FOURIER_GROUP_DIM ** -0.5
    attn_out_g = 1.0 + 0.02 * jax.random.normal(ks[5], (ATTN_WIDTH,), f32)
    fourier_out_g = 1.0 + 0.02 * jax.random.normal(ks[6], (FOURIER_WIDTH,), f32)
    w_out = jax.random.normal(ks[7], (MIX_WIDTH, D_MODEL), f32) * MIX_WIDTH ** -0.5
    norm2_g = 1.0 + 0.02 * jax.random.normal(ks[8], (D_MODEL,), f32)
    w_up = jax.random.normal(ks[9], (D_MODEL, 2 * D_FF), f32) * D_MODEL ** -0.5
    dw_w = jax.random.normal(ks[10], (CONV_WIDTH, 1, D_FF), f32) * CONV_WIDTH ** -0.5
    dw_b = 0.02 * jax.random.normal(ks[11], (D_FF,), f32)
    w_down = jax.random.normal(ks[12], (D_FF, D_MODEL), f32) * D_FF ** -0.5
    normf_g = 1.0 + 0.02 * jax.random.normal(ks[13], (D_MODEL,), f32)
    return {"x": x, "norm1_g": norm1_g, "w_in": w_in, "sink": sink, "w_fourier": w_fourier,
            "attn_out_g": attn_out_g, "fourier_out_g": fourier_out_g, "w_out": w_out,
            "norm2_g": norm2_g, "w_up": w_up, "dw_w": dw_w, "dw_b": dw_b, "w_down": w_down,
            "normf_g": normf_g}


def reference(x, norm1_g, w_in, sink, w_fourier, attn_out_g, fourier_out_g, w_out,
              norm2_g, w_up, dw_w, dw_b, w_down, normf_g):
    for _ in range(DEPTH):
        h = _rmsnorm(x, norm1_g)
        proj = h @ w_in
        q = proj[..., :ATTN_WIDTH]
        k = proj[..., ATTN_WIDTH:ATTN_WIDTH + KV_WIDTH]
        v = proj[..., ATTN_WIDTH + KV_WIDTH:ATTN_WIDTH + 2 * KV_WIDTH]
        u = proj[..., ATTN_WIDTH + 2 * KV_WIDTH:]
        a = _rmsnorm(_window_attention(q, k, v, sink), attn_out_g)
        f = _rmsnorm(_fourier_mix(u, w_fourier), fourier_out_g)
        x = x + jnp.concatenate([a, f], axis=-1) @ w_out
        x = x + _conv_ffn(_rmsnorm(x, norm2_g), w_up, dw_w, dw_b, w_down)
    return _rmsnorm(x, normf_g)
```

```python
import functools
import math

import numpy as np
import jax
import jax.numpy as jnp
from jax import lax
from jax.experimental import pallas as pl
from jax.experimental.pallas import tpu as pltpu

D_MODEL = 2048
SEQ = 8192
HEAD_DIM = 128
FOURIER_WIDTH = 512
ATTN_WIDTH = 1536
N_Q_HEADS = 12
GQA_GROUP = 3
N_KV_HEADS = 4
KV_WIDTH = 512
FOURIER_GROUP_DIM = 128
N_FOURIER_GROUPS = 4
IN_WIDTH = 3072
WINDOW = 128
BLOCK = 128
D_FF = 5504
EPS = 1e-6
NEG_INF = -1e30

VMEM_LIMIT_BYTES = 56 * 1024 * 1024

DFT_N1 = 128
DFT_N2 = SEQ // DFT_N1

ROW_TILE = 512
FF_CHUNK = 512
D_FF_PAD = ((D_FF + FF_CHUNK - 1) // FF_CHUNK) * FF_CHUNK
HALO_ROWS = 16


def _alibi_slopes(n_heads):
    def pow2_slopes(n):
        start = 2.0 ** (-8.0 / n)
        return [start ** (i + 1) for i in range(n)]
    if math.log2(n_heads).is_integer():
        return pow2_slopes(n_heads)
    closest = 2 ** int(math.floor(math.log2(n_heads)))
    return pow2_slopes(closest) + pow2_slopes(2 * closest)[0::2][: n_heads - closest]


def _rms_scale(v):
    return lax.rsqrt(jnp.mean(v * v, axis=-1, keepdims=True) + EPS)


def _in_proj_kernel(x_ref, g_ref, w_ref, q_ref, kv_ref, u_ref):
    x = x_ref[...]
    h = (x * _rms_scale(x) * g_ref[...]).astype(jnp.bfloat16)
    scale = HEAD_DIM ** -0.5
    cw = 512
    for c in range(IN_WIDTH // cw):
        r = jnp.dot(h, w_ref[:, c * cw:(c + 1) * cw], preferred_element_type=jnp.float32)
        if c < 3:
            q_ref[:, c * cw:(c + 1) * cw] = (r * scale).astype(jnp.bfloat16)
        elif c < 5:
            kv_ref[:, (c - 3) * cw:(c - 2) * cw] = r.astype(jnp.bfloat16)
        else:
            u_ref[...] = r


def _in_proj(x, g, w):
    n_tiles = SEQ // ROW_TILE
    return pl.pallas_call(
        _in_proj_kernel,
        grid=(n_tiles,),
        in_specs=[
            pl.BlockSpec((ROW_TILE, D_MODEL), lambda i: (i, 0)),
            pl.BlockSpec((1, D_MODEL), lambda i: (0, 0)),
            pl.BlockSpec((D_MODEL, IN_WIDTH), lambda i: (0, 0)),
        ],
        out_specs=[
            pl.BlockSpec((ROW_TILE, ATTN_WIDTH), lambda i: (i, 0)),
            pl.BlockSpec((ROW_TILE, 2 * KV_WIDTH), lambda i: (i, 0)),
            pl.BlockSpec((ROW_TILE, FOURIER_WIDTH), lambda i: (i, 0)),
        ],
        out_shape=[
            jax.ShapeDtypeStruct((SEQ, ATTN_WIDTH), jnp.bfloat16),
            jax.ShapeDtypeStruct((SEQ, 2 * KV_WIDTH), jnp.bfloat16),
            jax.ShapeDtypeStruct((SEQ, FOURIER_WIDTH), jnp.float32),
        ],
        compiler_params=pltpu.CompilerParams(
            dimension_semantics=("arbitrary",), vmem_limit_bytes=VMEM_LIMIT_BYTES),
        name="in_proj",
    )(x, g, w)


def _attention_kernel(sink_ref, q_ref, kvp_ref, kvc_ref, kvn_ref, g_ref, o_ref, acc_ref):
    i = pl.program_id(0)
    slopes = _alibi_slopes(N_Q_HEADS)
    qi = lax.broadcasted_iota(jnp.int32, (BLOCK, 3 * BLOCK), 0)
    kj = lax.broadcasted_iota(jnp.int32, (BLOCK, 3 * BLOCK), 1)
    arel = jnp.abs(kj - BLOCK - qi)
    s_abs = i * BLOCK - BLOCK + kj
    valid = (arel <= WINDOW) & (s_abs >= 0) & (s_abs < SEQ)
    base = jnp.where(valid, -arel.astype(jnp.float32), NEG_INF)

    for kh in range(N_KV_HEADS):
        ksl = slice(kh * HEAD_DIM, (kh + 1) * HEAD_DIM)
        vsl = slice(KV_WIDTH + kh * HEAD_DIM, KV_WIDTH + (kh + 1) * HEAD_DIM)
        kcat = jnp.concatenate([kvp_ref[:, ksl], kvc_ref[:, ksl], kvn_ref[:, ksl]], axis=0)
        vcat = jnp.concatenate([kvp_ref[:, vsl], kvc_ref[:, vsl], kvn_ref[:, vsl]], axis=0)
        qg = jnp.concatenate(
            [q_ref[:, (kh * GQA_GROUP + g) * HEAD_DIM:(kh * GQA_GROUP + g + 1) * HEAD_DIM]
             for g in range(GQA_GROUP)], axis=0)
        s = lax.dot_general(qg, kcat, (((1,), (1,)), ((), ())),
                            preferred_element_type=jnp.float32)
        ps, dens = [], []
        for g in range(GQA_GROUP):
            h = kh * GQA_GROUP + g
            sink_h = sink_ref[h]
            sg = s[g * BLOCK:(g + 1) * BLOCK] + slopes[h] * base
            m = jnp.maximum(jnp.max(sg, axis=-1, keepdims=True), sink_h)
            p = jnp.exp(sg - m)
            dens.append(jnp.sum(p, axis=-1, keepdims=True) + jnp.exp(sink_h - m))
            ps.append(p.astype(jnp.bfloat16))
        o = jnp.dot(jnp.concatenate(ps, axis=0), vcat, preferred_element_type=jnp.float32)
        for g in range(GQA_GROUP):
            h = kh * GQA_GROUP + g
            acc_ref[:, h * HEAD_DIM:(h + 1) * HEAD_DIM] = o[g * BLOCK:(g + 1) * BLOCK] / dens[g]

    a = acc_ref[...]
    o_ref[...] = (a * _rms_scale(a) * g_ref[...]).astype(jnp.bfloat16)


def _attention(sink, q, kv, g):
    nb = SEQ // BLOCK
    return pl.pallas_call(
        _attention_kernel,
        grid_spec=pltpu.PrefetchScalarGridSpec(
            num_scalar_prefetch=1,
            grid=(nb,),
            in_specs=[
                pl.BlockSpec((BLOCK, ATTN_WIDTH), lambda i, s: (i, 0)),
                pl.BlockSpec((BLOCK, 2 * KV_WIDTH), lambda i, s: (jnp.maximum(i - 1, 0), 0)),
                pl.BlockSpec((BLOCK, 2 * KV_WIDTH), lambda i, s: (i, 0)),
                pl.BlockSpec((BLOCK, 2 * KV_WIDTH), lambda i, s: (jnp.minimum(i + 1, nb - 1), 0)),
                pl.BlockSpec((1, ATTN_WIDTH), lambda i, s: (0, 0)),
            ],
            out_specs=pl.BlockSpec((BLOCK, ATTN_WIDTH), lambda i, s: (i, 0)),
            scratch_shapes=[pltpu.VMEM((BLOCK, ATTN_WIDTH), jnp.float32)],
        ),
        out_shape=jax.ShapeDtypeStruct((SEQ, ATTN_WIDTH), jnp.bfloat16),
        compiler_params=pltpu.CompilerParams(
            dimension_semantics=("arbitrary",), vmem_limit_bytes=VMEM_LIMIT_BYTES),
        name="attention",
    )(sink, q, kv, kv, kv, g)


def _dft_tables():
    n1 = np.arange(DFT_N1, dtype=np.float64)
    n2 = np.arange(DFT_N2, dtype=np.float64)
    ang1 = -2.0 * np.pi * (np.outer(n1, n1)[None] / DFT_N1 + (n2[:, None, None] * n1[None, :, None]) / SEQ)
    f1 = np.concatenate([np.cos(ang1), np.sin(ang1)], axis=1) / np.sqrt(DFT_N1)
    ang2 = -2.0 * np.pi * np.outer(n2, n2) / DFT_N2
    f2r, f2i = np.cos(ang2) / np.sqrt(DFT_N2), np.sin(ang2) / np.sqrt(DFT_N2)
    f2 = np.block([[f2r, -f2i], [f2i, f2r]])
    c = np.arange(FOURIER_GROUP_DIM, dtype=np.float64)
    angc = 2.0 * np.pi * np.outer(c, c) / FOURIER_GROUP_DIM
    cs = np.concatenate([np.cos(angc), np.sin(angc)], axis=0) / np.sqrt(FOURIER_GROUP_DIM)
    return (jnp.asarray(f1, dtype=jnp.bfloat16), jnp.asarray(f2, dtype=jnp.bfloat16),
            jnp.asarray(cs, dtype=jnp.bfloat16))


def _fourier_kernel(u_ref, f1_ref, f2_ref, cs_ref, w_ref, y_ref, yr_ref, yi_ref, xr_ref, xi_ref):
    C = FOURIER_GROUP_DIM

    def stage1(n2, carry):
        d = u_ref[pl.ds(n2, DFT_N1, stride=DFT_N2), :].astype(jnp.bfloat16)
        y = jnp.dot(f1_ref[n2], d, preferred_element_type=jnp.float32)
        row = pl.multiple_of(n2 * DFT_N1, DFT_N1)
        yr_ref[pl.ds(row, DFT_N1), :] = y[:DFT_N1]
        yi_ref[pl.ds(row, DFT_N1), :] = y[DFT_N1:]
        return carry
    lax.fori_loop(0, DFT_N2, stage1, 0)

    f2 = f2_ref[...]
    def stage2(k1, carry):
        dr = yr_ref[pl.ds(k1, DFT_N2, stride=DFT_N1), :]
        di = yi_ref[pl.ds(k1, DFT_N2, stride=DFT_N1), :]
        d = jnp.concatenate([dr, di], axis=0).astype(jnp.bfloat16)
        xc = jnp.dot(f2, d, preferred_element_type=jnp.float32)
        xr_ref[pl.ds(k1, DFT_N2, stride=DFT_N1), :] = xc[:DFT_N2]
        xi_ref[pl.ds(k1, DFT_N2, stride=DFT_N1), :] = xc[DFT_N2:]
        return carry
    lax.fori_loop(0, DFT_N1, stage2, 0)

    m = jnp.dot(cs_ref[...], w_ref[0].astype(jnp.bfloat16),
                preferred_element_type=jnp.float32).astype(jnp.bfloat16)
    rows = 512
    def stage3(t, carry):
        r0 = pl.multiple_of(t * rows, rows)
        xr = xr_ref[pl.ds(r0, rows), :].astype(jnp.bfloat16)
        xi = xi_ref[pl.ds(r0, rows), :].astype(jnp.bfloat16)
        y_ref[pl.ds(r0, rows), :] = (
            jnp.dot(xr, m[:C], preferred_element_type=jnp.float32)
            + jnp.dot(xi, m[C:], preferred_element_type=jnp.float32))
        return carry
    lax.fori_loop(0, SEQ // rows, stage3, 0)


def _fourier(u, w_fourier):
    f1, f2, cs = _dft_tables()
    C = FOURIER_GROUP_DIM
    return pl.pallas_call(
        _fourier_kernel,
        grid=(N_FOURIER_GROUPS,),
        in_specs=[
            pl.BlockSpec((SEQ, C), lambda g: (0, g)),
            pl.BlockSpec((DFT_N2, 2 * DFT_N1, DFT_N1), lambda g: (0, 0, 0)),
            pl.BlockSpec((2 * DFT_N2, 2 * DFT_N2), lambda g: (0, 0)),
            pl.BlockSpec((2 * C, C), lambda g: (0, 0)),
            pl.BlockSpec((1, C, C), lambda g: (g, 0, 0)),
        ],
        out_specs=pl.BlockSpec((SEQ, C), lambda g: (0, g)),
        out_shape=jax.ShapeDtypeStruct((SEQ, FOURIER_WIDTH), jnp.float32),
        scratch_shapes=[pltpu.VMEM((SEQ, C), jnp.float32) for _ in range(4)],
        compiler_params=pltpu.CompilerParams(
            dimension_semantics=("arbitrary",), vmem_limit_bytes=VMEM_LIMIT_BYTES),
        name="fourier",
    )(u, f1, f2, cs, w_fourier)


def _out_proj_kernel(a_ref, y_ref, gf_ref, w_ref, x_ref, g2_ref, x1_ref, h2_ref):
    y = y_ref[...]
    f = (y * _rms_scale(y) * gf_ref[...]).astype(jnp.bfloat16)
    a = a_ref[...]
    cw = 512
    ssq = jnp.zeros((ROW_TILE, 1), jnp.float32)
    for c in range(D_MODEL // cw):
        cs = slice(c * cw, (c + 1) * cw)
        r = (jnp.dot(a, w_ref[:ATTN_WIDTH, cs], preferred_element_type=jnp.float32)
             + jnp.dot(f, w_ref[ATTN_WIDTH:, cs], preferred_element_type=jnp.float32))
        x1 = x_ref[:, cs] + r
        x1_ref[:, cs] = x1
        ssq = ssq + jnp.sum(x1 * x1, axis=-1, keepdims=True)
    rs = lax.rsqrt(ssq * (1.0 / D_MODEL) + EPS)
    h2_ref[...] = (x1_ref[...] * rs * g2_ref[...]).astype(jnp.bfloat16)


def _out_proj(a, y, gf, w, x, g2):
    n_tiles = SEQ // ROW_TILE
    return pl.pallas_call(
        _out_proj_kernel,
        grid=(n_tiles,),
        in_specs=[
            pl.BlockSpec((ROW_TILE, ATTN_WIDTH), lambda i: (i, 0)),
            pl.BlockSpec((ROW_TILE, FOURIER_WIDTH), lambda i: (i, 0)),
            pl.BlockSpec((1, FOURIER_WIDTH), lambda i: (0, 0)),
            pl.BlockSpec((D_MODEL, D_MODEL), lambda i: (0, 0)),
            pl.BlockSpec((ROW_TILE, D_MODEL), lambda i: (i, 0)),
            pl.BlockSpec((1, D_MODEL), lambda i: (0, 0)),
        ],
        out_specs=[
            pl.BlockSpec((ROW_TILE, D_MODEL), lambda i: (i, 0)),
            pl.BlockSpec((ROW_TILE, D_MODEL), lambda i: (i, 0)),
        ],
        out_shape=[
            jax.ShapeDtypeStruct((SEQ, D_MODEL), jnp.float32),
            jax.ShapeDtypeStruct((SEQ, D_MODEL), jnp.bfloat16),
        ],
        compiler_params=pltpu.CompilerParams(
            dimension_semantics=("arbitrary",), vmem_limit_bytes=VMEM_LIMIT_BYTES),
        name="out_proj",
    )(a, y, gf, w, x, g2)


def _gelu(v):
    return 0.5 * v * (1.0 + lax.erf(v * (2.0 ** -0.5)))


def _conv_ffn_kernel(h_ref, hp_ref, hn_ref, wg_ref, wv_ref, dww_ref, dwb_ref, wd_ref,
                     x1_ref, gf_ref, o_ref, acc_ref):
    i = pl.program_id(0)
    c = pl.program_id(1)

    @pl.when(c == 0)
    def _():
        acc_ref[...] = jnp.zeros_like(acc_ref)

    h = h_ref[...]
    wg = wg_ref[...]
    gate = jnp.dot(h, wg, preferred_element_type=jnp.float32)
    val = jnp.dot(h, wv_ref[...], preferred_element_type=jnp.float32)
    gp = jnp.dot(hp_ref[...], wg, preferred_element_type=jnp.float32)[HALO_ROWS - 1:HALO_ROWS]
    gn = jnp.dot(hn_ref[...], wg, preferred_element_type=jnp.float32)[0:1]
    gp = jnp.where(i == 0, 0.0, gp)
    gn = jnp.where(i == pl.num_programs(0) - 1, 0.0, gn)

    row = lax.broadcasted_iota(jnp.int32, gate.shape, 0)
    g_prev = jnp.where(row == 0, gp, pltpu.roll(gate, 1, 0))
    g_next = jnp.where(row == ROW_TILE - 1, gn, pltpu.roll(gate, ROW_TILE - 1, 0))
    w = dww_ref[...]
    conv = w[0:1] * g_prev + w[1:2] * gate + w[2:3] * g_next + dwb_ref[...]
    act = (_gelu(conv) * val).astype(jnp.bfloat16)
    acc_ref[...] += jnp.dot(act, wd_ref[...], preferred_element_type=jnp.float32)

    @pl.when(c == pl.num_programs(1) - 1)
    def _():
        o = x1_ref[...] + acc_ref[...]
        o_ref[...] = o * _rms_scale(o) * gf_ref[...]


def _conv_ffn(h2, wg, wv, dww, dwb, wd, x1, gf):
    n_tiles = SEQ // ROW_TILE
    n_chunks = D_FF_PAD // FF_CHUNK
    halo_per_tile = ROW_TILE // HALO_ROWS
    n_halo = SEQ // HALO_ROWS
    return pl.pallas_call(
        _conv_ffn_kernel,
        grid=(n_tiles, n_chunks),
        in_specs=[
            pl.BlockSpec((ROW_TILE, D_MODEL), lambda i, c: (i, 0)),
            pl.BlockSpec((HALO_ROWS, D_MODEL),
                         lambda i, c: (jnp.maximum(i * halo_per_tile - 1, 0), 0)),
            pl.BlockSpec((HALO_ROWS, D_MODEL),
                         lambda i, c: (jnp.minimum((i + 1) * halo_per_tile, n_halo - 1), 0)),
            pl.BlockSpec((D_MODEL, FF_CHUNK), lambda i, c: (0, c)),
            pl.BlockSpec((D_MODEL, FF_CHUNK), lambda i, c: (0, c)),
            pl.BlockSpec((3, FF_CHUNK), lambda i, c: (0, c)),
            pl.BlockSpec((1, FF_CHUNK), lambda i, c: (0, c)),
            pl.BlockSpec((FF_CHUNK, D_MODEL), lambda i, c: (c, 0)),
            pl.BlockSpec((ROW_TILE, D_MODEL), lambda i, c: (i, 0)),
            pl.BlockSpec((1, D_MODEL), lambda i, c: (0, 0)),
        ],
        out_specs=pl.BlockSpec((ROW_TILE, D_MODEL), lambda i, c: (i, 0)),
        out_shape=jax.ShapeDtypeStruct((SEQ, D_MODEL), jnp.float32),
        scratch_shapes=[pltpu.VMEM((ROW_TILE, D_MODEL), jnp.float32)],
        compiler_params=pltpu.CompilerParams(
            dimension_semantics=("arbitrary", "arbitrary"), vmem_limit_bytes=VMEM_LIMIT_BYTES),
        name="conv_ffn",
    )(h2, h2, h2, wg, wv, dww, dwb, wd, x1, gf)


def kernel(x, norm1_g, w_in, sink, w_fourier, attn_out_g, fourier_out_g, w_out, norm2_g,
           w_up, dw_w, dw_b, w_down, normf_g):
    bf16 = jnp.bfloat16
    x2 = x.reshape(SEQ, D_MODEL)
    pad = D_FF_PAD - D_FF
    wg = jnp.pad(w_up[:, :D_FF].astype(bf16), ((0, 0), (0, pad)))
    wv = jnp.pad(w_up[:, D_FF:].astype(bf16), ((0, 0), (0, pad)))
    wd = jnp.pad(w_down.astype(bf16), ((0, pad), (0, 0)))
    dww = jnp.pad(dw_w.reshape(3, D_FF), ((0, 0), (0, pad)))
    dwb = jnp.pad(dw_b.reshape(1, D_FF), ((0, 0), (0, pad)))

    q, kv, u = _in_proj(x2, norm1_g.reshape(1, D_MODEL), w_in.astype(bf16))
    a = _attention(sink, q, kv, attn_out_g.reshape(1, ATTN_WIDTH))
    y = _fourier(u, w_fourier)
    x1, h2 = _out_proj(a, y, fourier_out_g.reshape(1, FOURIER_WIDTH), w_out.astype(bf16),
                       x2, norm2_g.reshape(1, D_MODEL))
    out = _conv_ffn(h2, wg, wv, dww, dwb, wd, x1, normf_g.reshape(1, D_MODEL))
    return out.reshape(x.shape)
```

```python
import math

import numpy as np
import jax
import jax.numpy as jnp
from jax import lax
from jax.experimental import pallas as pl
from jax.experimental.pallas import tpu as pltpu

D_MODEL = 2048
SEQ = 8192
HEAD_DIM = 128
FOURIER_WIDTH = 512
ATTN_WIDTH = 1536
N_Q_HEADS = 12
GQA_GROUP = 3
N_KV_HEADS = 4
KV_WIDTH = 512
FOURIER_GROUP_DIM = 128
N_FOURIER_GROUPS = 4
IN_WIDTH = 3072
WINDOW = 128
BLOCK = 128
D_FF = 5504
EPS = 1e-6
NEG_INF = -1e30

VMEM_LIMIT_BYTES = 60 * 1024 * 1024
SUBLANES = 8

DFT_N1 = 128
DFT_N2 = SEQ // DFT_N1

ROW_TILE = 512
FF_ROW_TILE = 1024
FF_CHUNK = 256
FF_ROW_BLOCK = 128
X1_PANEL = 256
N_FF_CHUNKS = -(-D_FF // FF_CHUNK)
HALO_ROWS = 16


def _alibi_slopes(n_heads):
    def pow2_slopes(n):
        start = 2.0 ** (-8.0 / n)
        return [start ** (i + 1) for i in range(n)]
    if math.log2(n_heads).is_integer():
        s = pow2_slopes(n_heads)
    else:
        closest = 2 ** int(math.floor(math.log2(n_heads)))
        s = pow2_slopes(closest) + pow2_slopes(2 * closest)[0::2][: n_heads - closest]
    return [float(np.float32(v)) for v in s]


def _rms_scale(v):
    return lax.rsqrt(jnp.mean(v * v, axis=-1, keepdims=True) + EPS)


def _params(*semantics):
    return pltpu.CompilerParams(dimension_semantics=semantics, vmem_limit_bytes=VMEM_LIMIT_BYTES)


def _in_proj_kernel(x_ref, g_ref, w_ref, q_ref, kv_ref, u_ref):
    x = x_ref[...]
    h = (x * _rms_scale(x) * g_ref[...]).astype(jnp.bfloat16)
    scale = HEAD_DIM ** -0.5
    cw = 512
    for c in range(IN_WIDTH // cw):
        r = jnp.dot(h, w_ref[:, c * cw:(c + 1) * cw], preferred_element_type=jnp.float32)
        if c < 3:
            q_ref[:, c * cw:(c + 1) * cw] = (r * scale).astype(jnp.bfloat16)
        elif c < 5:
            kv_ref[:, (c - 3) * cw:(c - 2) * cw] = r.astype(jnp.bfloat16)
        else:
            n1_per_tile = ROW_TILE // DFT_N2
            u3d = u_ref.reshape(N_FOURIER_GROUPS, ROW_TILE, FOURIER_GROUP_DIM)
            for g in range(N_FOURIER_GROUPS):
                for n1l in range(n1_per_tile):
                    u3d[g, pl.ds(n1l, DFT_N2, stride=n1_per_tile), :] = r[
                        n1l * DFT_N2:(n1l + 1) * DFT_N2,
                        g * FOURIER_GROUP_DIM:(g + 1) * FOURIER_GROUP_DIM]


def _in_proj(x, g, w):
    n_tiles = SEQ // ROW_TILE
    return pl.pallas_call(
        _in_proj_kernel,
        grid=(n_tiles,),
        in_specs=[
            pl.BlockSpec((ROW_TILE, D_MODEL), lambda i: (i, 0)),
            pl.BlockSpec((1, D_MODEL), lambda i: (0, 0)),
            pl.BlockSpec((D_MODEL, IN_WIDTH), lambda i: (0, 0)),
        ],
        out_specs=[
            pl.BlockSpec((ROW_TILE, ATTN_WIDTH), lambda i: (i, 0)),
            pl.BlockSpec((ROW_TILE, 2 * KV_WIDTH), lambda i: (i, 0)),
            pl.BlockSpec((N_FOURIER_GROUPS, DFT_N2, ROW_TILE // DFT_N2, FOURIER_GROUP_DIM),
                         lambda i: (0, 0, i, 0)),
        ],
        out_shape=[
            jax.ShapeDtypeStruct((SEQ, ATTN_WIDTH), jnp.bfloat16),
            jax.ShapeDtypeStruct((SEQ, 2 * KV_WIDTH), jnp.bfloat16),
            jax.ShapeDtypeStruct((N_FOURIER_GROUPS, DFT_N2, DFT_N1, FOURIER_GROUP_DIM), jnp.float32),
        ],
        compiler_params=_params("arbitrary"),
        name="in_proj",
    )(x, g, w)


def _attention_kernel(sink_ref, q_ref, kvp_ref, kvc_ref, kvn_ref, g_ref, o_ref, acc_ref):
    i = pl.program_id(0)
    slopes = _alibi_slopes(N_Q_HEADS)
    qi = lax.broadcasted_iota(jnp.int32, (BLOCK, 3 * BLOCK), 0)
    kj = lax.broadcasted_iota(jnp.int32, (BLOCK, 3 * BLOCK), 1)
    arel = jnp.abs(kj - BLOCK - qi)
    s_abs = i * BLOCK - BLOCK + kj
    valid = (arel <= WINDOW) & (s_abs >= 0) & (s_abs < SEQ)
    base = jnp.where(valid, -arel.astype(jnp.float32), NEG_INF)

    for kh in range(N_KV_HEADS):
        ksl = slice(kh * HEAD_DIM, (kh + 1) * HEAD_DIM)
        vsl = slice(KV_WIDTH + kh * HEAD_DIM, KV_WIDTH + (kh + 1) * HEAD_DIM)
        kcat = jnp.concatenate([kvp_ref[:, ksl], kvc_ref[:, ksl], kvn_ref[:, ksl]], axis=0)
        vcat = jnp.concatenate([kvp_ref[:, vsl], kvc_ref[:, vsl], kvn_ref[:, vsl]], axis=0)
        qg = jnp.concatenate(
            [q_ref[:, (kh * GQA_GROUP + g) * HEAD_DIM:(kh * GQA_GROUP + g + 1) * HEAD_DIM]
             for g in range(GQA_GROUP)], axis=0)
        s = lax.dot_general(qg, kcat, (((1,), (1,)), ((), ())),
                            preferred_element_type=jnp.float32)
        ps, dens = [], []
        for g in range(GQA_GROUP):
            h = kh * GQA_GROUP + g
            sink_h = sink_ref[h]
            sg = s[g * BLOCK:(g + 1) * BLOCK] + slopes[h] * base
            m = jnp.maximum(jnp.max(sg, axis=-1, keepdims=True), sink_h)
            p = jnp.exp(sg - m)
            dens.append(jnp.sum(p, axis=-1, keepdims=True) + jnp.exp(sink_h - m))
            ps.append(p.astype(jnp.bfloat16))
        o = jnp.dot(jnp.concatenate(ps, axis=0), vcat, preferred_element_type=jnp.float32)
        for g in range(GQA_GROUP):
            h = kh * GQA_GROUP + g
            acc_ref[:, h * HEAD_DIM:(h + 1) * HEAD_DIM] = o[g * BLOCK:(g + 1) * BLOCK] / dens[g]

    a = acc_ref[...]
    o_ref[...] = (a * _rms_scale(a) * g_ref[...]).astype(jnp.bfloat16)


def _attention(sink, q, kv, g):
    nb = SEQ // BLOCK
    return pl.pallas_call(
        _attention_kernel,
        grid_spec=pltpu.PrefetchScalarGridSpec(
            num_scalar_prefetch=1,
            grid=(nb,),
            in_specs=[
                pl.BlockSpec((BLOCK, ATTN_WIDTH), lambda i, s: (i, 0)),
                pl.BlockSpec((BLOCK, 2 * KV_WIDTH), lambda i, s: (jnp.maximum(i - 1, 0), 0)),
                pl.BlockSpec((BLOCK, 2 * KV_WIDTH), lambda i, s: (i, 0)),
                pl.BlockSpec((BLOCK, 2 * KV_WIDTH), lambda i, s: (jnp.minimum(i + 1, nb - 1), 0)),
                pl.BlockSpec((1, ATTN_WIDTH), lambda i, s: (0, 0)),
            ],
            out_specs=pl.BlockSpec((BLOCK, ATTN_WIDTH), lambda i, s: (i, 0)),
            scratch_shapes=[pltpu.VMEM((BLOCK, ATTN_WIDTH), jnp.float32)],
        ),
        out_shape=jax.ShapeDtypeStruct((SEQ, ATTN_WIDTH), jnp.bfloat16),
        compiler_params=_params("arbitrary"),
        name="attention",
    )(sink, q, kv, kv, kv, g)


def _dft_tables():
    n1 = np.arange(DFT_N1, dtype=np.float64)
    n2 = np.arange(DFT_N2, dtype=np.float64)
    ang1 = -2.0 * np.pi * (np.outer(n1, n1)[None] / DFT_N1 + (n2[:, None, None] * n1[None, :, None]) / SEQ)
    f1 = np.concatenate([np.cos(ang1), np.sin(ang1)], axis=1) / np.sqrt(DFT_N1)
    ang2 = -2.0 * np.pi * np.outer(n2, n2) / DFT_N2
    f2r, f2i = np.cos(ang2) / np.sqrt(DFT_N2), np.sin(ang2) / np.sqrt(DFT_N2)
    f2 = np.block([[f2r, -f2i], [f2i, f2r]])
    c = np.arange(FOURIER_GROUP_DIM, dtype=np.float64)
    angc = 2.0 * np.pi * np.outer(c, c) / FOURIER_GROUP_DIM
    cs = np.concatenate([np.cos(angc), np.sin(angc)], axis=0) / np.sqrt(FOURIER_GROUP_DIM)
    return (jnp.asarray(f1, dtype=jnp.float32), jnp.asarray(f2, dtype=jnp.float32),
            jnp.asarray(cs, dtype=jnp.float32))


DFT_BATCH = SUBLANES


def _group_lanes(ref, rows):
    return jnp.concatenate([ref[g, rows, :] for g in range(N_FOURIER_GROUPS)], axis=1)


def _dft_stage1_kernel(u_ref, f1_ref, yr_ref, yi_ref):
    G, C = N_FOURIER_GROUPS, FOURIER_GROUP_DIM
    yr3d = yr_ref.reshape(G, DFT_N1 * DFT_BATCH, C)
    yi3d = yi_ref.reshape(G, DFT_N1 * DFT_BATCH, C)
    for j in range(DFT_BATCH):
        d = _group_lanes(u_ref, slice(j * DFT_N1, (j + 1) * DFT_N1)).astype(jnp.bfloat16)
        y = jnp.dot(f1_ref[j].astype(jnp.bfloat16), d, preferred_element_type=jnp.float32)
        for g in range(G):
            yr3d[g, pl.ds(j, DFT_N1, stride=DFT_BATCH), :] = y[:DFT_N1, g * C:(g + 1) * C]
            yi3d[g, pl.ds(j, DFT_N1, stride=DFT_BATCH), :] = y[DFT_N1:, g * C:(g + 1) * C]


def _dft_stage1(u, f1):
    G, C = N_FOURIER_GROUPS, FOURIER_GROUP_DIM
    n_steps = DFT_N2 // DFT_BATCH
    y_shape = jax.ShapeDtypeStruct((G, DFT_N1, DFT_N2, C), jnp.float32)
    y_spec = pl.BlockSpec((G, DFT_N1, DFT_BATCH, C), lambda s: (0, 0, s, 0))
    return pl.pallas_call(
        _dft_stage1_kernel,
        grid=(n_steps,),
        in_specs=[
            pl.BlockSpec((G, DFT_BATCH * DFT_N1, C), lambda s: (0, s, 0)),
            pl.BlockSpec((DFT_BATCH, 2 * DFT_N1, DFT_N1), lambda s: (s, 0, 0)),
        ],
        out_specs=[y_spec, y_spec],
        out_shape=[y_shape, y_shape],
        compiler_params=_params("arbitrary"),
        name="dft_stage1",
    )(u, f1)


def _dft_stage2_kernel(yr_ref, yi_ref, f2_ref, cs_ref, w_ref, y_ref, x_ref, m_ref):
    G, C = N_FOURIER_GROUPS, FOURIER_GROUP_DIM

    @pl.when(pl.program_id(0) == 0)
    def _():
        for g in range(G):
            m_ref[g] = jnp.dot(cs_ref[...].astype(jnp.bfloat16), w_ref[g].astype(jnp.bfloat16),
                               preferred_element_type=jnp.float32).astype(jnp.bfloat16)

    f2 = f2_ref[...].astype(jnp.bfloat16)
    for j in range(DFT_BATCH):
        rows = slice(j * DFT_N2, (j + 1) * DFT_N2)
        d = jnp.concatenate([_group_lanes(yr_ref, rows), _group_lanes(yi_ref, rows)],
                            axis=0).astype(jnp.bfloat16)
        xc = jnp.dot(f2, d, preferred_element_type=jnp.float32)
        for g in range(G):
            x_ref[g, pl.ds(j, DFT_N2, stride=DFT_BATCH), :] = xc[:DFT_N2, g * C:(g + 1) * C]
            x_ref[G + g, pl.ds(j, DFT_N2, stride=DFT_BATCH), :] = xc[DFT_N2:, g * C:(g + 1) * C]

    for g in range(G):
        m = m_ref[g]
        y_ref[g] = (
            jnp.dot(x_ref[g].astype(jnp.bfloat16), m[:C], preferred_element_type=jnp.float32)
            + jnp.dot(x_ref[G + g].astype(jnp.bfloat16), m[C:], preferred_element_type=jnp.float32)
        ).reshape(DFT_N2, DFT_BATCH, C)


def _dft_stage2(yr, yi, f2, cs, w_fourier):
    G, C = N_FOURIER_GROUPS, FOURIER_GROUP_DIM
    n_steps = DFT_N1 // DFT_BATCH
    y_spec = pl.BlockSpec((G, DFT_BATCH * DFT_N2, C), lambda s: (0, s, 0))
    return pl.pallas_call(
        _dft_stage2_kernel,
        grid=(n_steps,),
        in_specs=[
            y_spec, y_spec,
            pl.BlockSpec((2 * DFT_N2, 2 * DFT_N2), lambda s: (0, 0)),
            pl.BlockSpec((2 * C, C), lambda s: (0, 0)),
            pl.BlockSpec((G, C, C), lambda s: (0, 0, 0)),
        ],
        out_specs=pl.BlockSpec((G, DFT_N2, DFT_BATCH, C), lambda s: (0, 0, s, 0)),
        out_shape=jax.ShapeDtypeStruct((G, DFT_N2, DFT_N1, C), jnp.float32),
        scratch_shapes=[
            pltpu.VMEM((2 * G, DFT_BATCH * DFT_N2, C), jnp.float32),
            pltpu.VMEM((G, 2 * C, C), jnp.bfloat16),
        ],
        compiler_params=_params("arbitrary"),
        name="dft_stage2",
    )(yr, yi, f2, cs, w_fourier)


def _fourier(u_sw, w_fourier):
    G, C = N_FOURIER_GROUPS, FOURIER_GROUP_DIM
    f1, f2, cs = _dft_tables()
    yr, yi = _dft_stage1(u_sw.reshape(G, SEQ, C), f1)
    y = _dft_stage2(yr.reshape(G, SEQ, C), yi.reshape(G, SEQ, C), f2, cs, w_fourier)
    return y.reshape(G, SEQ, C)


def _out_proj_kernel(a_ref, y_ref, gf_ref, w_ref, x_ref, g2_ref, x1_ref, h2_ref):
    y = _group_lanes(y_ref, slice(None))
    f = (y * _rms_scale(y) * gf_ref[...]).astype(jnp.bfloat16)
    a = a_ref[...]
    cw = 512
    ssq = jnp.zeros((ROW_TILE, 1), jnp.float32)
    for c in range(D_MODEL // cw):
        cs = slice(c * cw, (c + 1) * cw)
        r = (jnp.dot(a, w_ref[:ATTN_WIDTH, cs], preferred_element_type=jnp.float32)
             + jnp.dot(f, w_ref[ATTN_WIDTH:, cs], preferred_element_type=jnp.float32))
        x1 = x_ref[:, cs] + r
        x1_ref[:, cs] = x1
        ssq = ssq + jnp.sum(x1 * x1, axis=-1, keepdims=True)
    rs = lax.rsqrt(ssq * (1.0 / D_MODEL) + EPS)
    h2_ref[...] = (x1_ref[...] * rs * g2_ref[...]).astype(jnp.bfloat16)


def _out_proj(a, y, gf, w, x, g2):
    n_tiles = SEQ // ROW_TILE
    return pl.pallas_call(
        _out_proj_kernel,
        grid=(n_tiles,),
        in_specs=[
            pl.BlockSpec((ROW_TILE, ATTN_WIDTH), lambda i: (i, 0)),
            pl.BlockSpec((N_FOURIER_GROUPS, ROW_TILE, FOURIER_GROUP_DIM), lambda i: (0, i, 0)),
            pl.BlockSpec((1, FOURIER_WIDTH), lambda i: (0, 0)),
            pl.BlockSpec((D_MODEL, D_MODEL), lambda i: (0, 0)),
            pl.BlockSpec((ROW_TILE, D_MODEL), lambda i: (i, 0)),
            pl.BlockSpec((1, D_MODEL), lambda i: (0, 0)),
        ],
        out_specs=[
            pl.BlockSpec((ROW_TILE, D_MODEL), lambda i: (i, 0)),
            pl.BlockSpec((ROW_TILE, D_MODEL), lambda i: (i, 0)),
        ],
        out_shape=[
            jax.ShapeDtypeStruct((SEQ, D_MODEL), jnp.float32),
            jax.ShapeDtypeStruct((SEQ, D_MODEL), jnp.bfloat16),
        ],
        compiler_params=_params("arbitrary"),
        name="out_proj",
    )(a, y, gf, w, x, g2)


def _gelu(v):
    return 0.5 * v * (1.0 + lax.erf(v * (2.0 ** -0.5)))


def _ff_chunk_start(c):
    return pl.multiple_of(jnp.minimum(c * FF_CHUNK, D_FF - FF_CHUNK), 128)


def _ffn_step(chunk, hx_ref, wg_ref, wv_ref, up_refs, dww_ref, dwb_ref, wd_ref, down_refs, o_ref):
    H, T, B = HALO_ROWS, FF_ROW_TILE, FF_ROW_BLOCK
    n_blocks = T // B
    if up_refs is not None:
        wg = wg_ref[...].astype(jnp.bfloat16)
        wv = wv_ref[...].astype(jnp.bfloat16)
        gate_up, val_up = up_refs
        gate_edges = [(T + 2 * H) * b // n_blocks // H * H for b in range(n_blocks + 1)]
    gate_dn, val_dn = down_refs
    w = dww_ref[...]
    wd = wd_ref[...].astype(jnp.bfloat16)
    overlap = chunk * FF_CHUNK - _ff_chunk_start(chunk)
    keep = lax.broadcasted_iota(jnp.int32, (1, FF_CHUNK), 1) >= overlap
    for b in range(n_blocks):
        if up_refs is not None:
            g0, g1 = gate_edges[b], gate_edges[b + 1]
            gate_up[g0:g1] = jnp.dot(hx_ref[g0:g1], wg, preferred_element_type=jnp.float32)
            val_up[b * B:(b + 1) * B] = jnp.dot(hx_ref[H + b * B:H + (b + 1) * B], wv,
                                                preferred_element_type=jnp.float32)
        r0, r1 = b * B, (b + 1) * B
        conv = (w[0:1] * gate_dn[H + r0 - 1:H + r1 - 1] + w[1:2] * gate_dn[H + r0:H + r1]
                + w[2:3] * gate_dn[H + r0 + 1:H + r1 + 1] + dwb_ref[...])
        act = jnp.where(keep, _gelu(conv) * val_dn[r0:r1], 0.0).astype(jnp.bfloat16)
        o_ref[r0:r1] += jnp.dot(act, wd, preferred_element_type=jnp.float32)


def _conv_ffn_kernel(h_ref, hp_ref, hn_ref, wg_ref, wv_ref, dww_ref, dwb_ref, wd_ref,
                     x1_ref, gf_ref, o_ref, hx_ref, gate0_ref, val0_ref, gate1_ref, val1_ref):
    i = pl.program_id(0)
    c = pl.program_id(1)
    T = FF_ROW_TILE
    H = HALO_ROWS
    prev = jnp.maximum(c - 1, 0)

    @pl.when(c == 0)
    def _():
        hx_ref[0:H] = jnp.where(i == 0, jnp.zeros_like(hp_ref), hp_ref[...])
        hx_ref[H:H + T] = h_ref[...]
        hx_ref[H + T:] = jnp.where(i == pl.num_programs(0) - 1, jnp.zeros_like(hn_ref), hn_ref[...])
        o_ref[...] = jnp.zeros_like(o_ref)
        gate1_ref[...] = jnp.zeros_like(gate1_ref)
        val1_ref[...] = jnp.zeros_like(val1_ref)

    for k in range(D_MODEL // X1_PANEL):
        @pl.when(c == k)
        def _(k=k):
            o_ref[:, k * X1_PANEL:(k + 1) * X1_PANEL] += x1_ref[...]

    slot0, slot1 = (gate0_ref, val0_ref), (gate1_ref, val1_ref)

    @pl.when((c < N_FF_CHUNKS) & (c % 2 == 0))
    def _():
        _ffn_step(prev, hx_ref, wg_ref, wv_ref, slot0, dww_ref, dwb_ref, wd_ref, slot1, o_ref)

    @pl.when((c < N_FF_CHUNKS) & (c % 2 == 1))
    def _():
        _ffn_step(prev, hx_ref, wg_ref, wv_ref, slot1, dww_ref, dwb_ref, wd_ref, slot0, o_ref)

    @pl.when(c == N_FF_CHUNKS)
    def _():
        last = slot0 if (N_FF_CHUNKS - 1) % 2 == 0 else slot1
        _ffn_step(prev, hx_ref, wg_ref, wv_ref, None, dww_ref, dwb_ref, wd_ref, last, o_ref)
        o = o_ref[...]
        o_ref[...] = o * _rms_scale(o) * gf_ref[...]


def _conv_ffn(h2, w_up, dww, dwb, w_down, x1, gf):
    n_tiles = SEQ // FF_ROW_TILE
    halo_per_tile = FF_ROW_TILE // HALO_ROWS
    n_halo = SEQ // HALO_ROWS
    E = pl.Element
    T, H = FF_ROW_TILE, HALO_ROWS

    def up_start(c):
        return _ff_chunk_start(jnp.minimum(c, N_FF_CHUNKS - 1))

    def down_start(c):
        return _ff_chunk_start(jnp.maximum(c - 1, 0))

    return pl.pallas_call(
        _conv_ffn_kernel,
        grid=(n_tiles, N_FF_CHUNKS + 1),
        in_specs=[
            pl.BlockSpec((T, D_MODEL), lambda i, c: (i, 0)),
            pl.BlockSpec((H, D_MODEL), lambda i, c: (jnp.maximum(i * halo_per_tile - 1, 0), 0)),
            pl.BlockSpec((H, D_MODEL),
                         lambda i, c: (jnp.minimum((i + 1) * halo_per_tile, n_halo - 1), 0)),
            pl.BlockSpec((E(D_MODEL), E(FF_CHUNK)), lambda i, c: (0, up_start(c))),
            pl.BlockSpec((E(D_MODEL), E(FF_CHUNK)),
                         lambda i, c: (0, pl.multiple_of(D_FF + up_start(c), 128))),
            pl.BlockSpec((E(3), E(FF_CHUNK)), lambda i, c: (0, down_start(c))),
            pl.BlockSpec((E(1), E(FF_CHUNK)), lambda i, c: (0, down_start(c))),
            pl.BlockSpec((E(FF_CHUNK), E(D_MODEL)), lambda i, c: (down_start(c), 0)),
            pl.BlockSpec((T, X1_PANEL),
                         lambda i, c: (i, jnp.minimum(c, D_MODEL // X1_PANEL - 1))),
            pl.BlockSpec((1, D_MODEL), lambda i, c: (0, 0)),
        ],
        out_specs=pl.BlockSpec((T, D_MODEL), lambda i, c: (i, 0)),
        out_shape=jax.ShapeDtypeStruct((SEQ, D_MODEL), jnp.float32),
        scratch_shapes=[
            pltpu.VMEM((T + 2 * H, D_MODEL), jnp.bfloat16),
            pltpu.VMEM((T + 2 * H, FF_CHUNK), jnp.float32), pltpu.VMEM((T, FF_CHUNK), jnp.float32),
            pltpu.VMEM((T + 2 * H, FF_CHUNK), jnp.float32), pltpu.VMEM((T, FF_CHUNK), jnp.float32),
        ],
        compiler_params=_params("arbitrary", "arbitrary"),
        name="conv_ffn",
    )(h2, h2, h2, w_up, w_up, dww, dwb, w_down, x1, gf)


def kernel(x, norm1_g, w_in, sink, w_fourier, attn_out_g, fourier_out_g, w_out, norm2_g,
           w_up, dw_w, dw_b, w_down, normf_g):
    bf16 = jnp.bfloat16
    x2 = x.reshape(SEQ, D_MODEL)
    q, kv, u_sw = _in_proj(x2, norm1_g.reshape(1, D_MODEL), w_in.astype(bf16))
    a = _attention(sink, q, kv, attn_out_g.reshape(1, ATTN_WIDTH))
    y = _fourier(u_sw, w_fourier)
    x1, h2 = _out_proj(a, y, fourier_out_g.reshape(1, FOURIER_WIDTH), w_out.astype(bf16),
                       x2, norm2_g.reshape(1, D_MODEL))
    out = _conv_ffn(h2, w_up, dw_w.reshape(3, D_FF), dw_b.reshape(1, D_FF), w_down, x1,
                    normf_g.reshape(1, D_MODEL))
    return out.reshape(x.shape)
```

```python
import math

import numpy as np
import jax
import jax.numpy as jnp
from jax import lax
from jax.experimental import pallas as pl
from jax.experimental.pallas import tpu as pltpu

D_MODEL = 2048
SEQ = 8192
HEAD_DIM = 128
FOURIER_WIDTH = 512
ATTN_WIDTH = 1536
N_Q_HEADS = 12
GQA_GROUP = 3
N_KV_HEADS = 4
KV_WIDTH = 512
FOURIER_GROUP_DIM = 128
N_FOURIER_GROUPS = 4
IN_WIDTH = 3072
WINDOW = 128
BLOCK = 128
D_FF = 5504
EPS = 1e-6
NEG_INF = -1e30

VMEM_LIMIT_BYTES = 60 * 1024 * 1024
SUBLANES = 8

DFT_N1 = 128
DFT_N2 = SEQ // DFT_N1

ROW_TILE = 512
FF_ROW_TILE = 1024
FF_CHUNK = 512
FF_ROW_BLOCK = 128
X1_PANEL = 256
N_FF_CHUNKS = -(-D_FF // FF_CHUNK)
D_FF_PAD = N_FF_CHUNKS * FF_CHUNK
HALO_ROWS = 16


def _alibi_slopes(n_heads):
    def pow2_slopes(n):
        start = 2.0 ** (-8.0 / n)
        return [start ** (i + 1) for i in range(n)]
    if math.log2(n_heads).is_integer():
        s = pow2_slopes(n_heads)
    else:
        closest = 2 ** int(math.floor(math.log2(n_heads)))
        s = pow2_slopes(closest) + pow2_slopes(2 * closest)[0::2][: n_heads - closest]
    return [float(np.float32(v)) for v in s]


def _rms_scale(v):
    return lax.rsqrt(jnp.mean(v * v, axis=-1, keepdims=True) + EPS)


def _params(*semantics):
    return pltpu.CompilerParams(dimension_semantics=semantics, vmem_limit_bytes=VMEM_LIMIT_BYTES)


def _in_proj_kernel(x_ref, g_ref, w_ref, q_ref, kv_ref, u_ref):
    x = x_ref[...]
    h = (x * _rms_scale(x) * g_ref[...]).astype(jnp.bfloat16)
    scale = HEAD_DIM ** -0.5
    cw = 512
    for c in range(IN_WIDTH // cw):
        r = jnp.dot(h, w_ref[:, c * cw:(c + 1) * cw], preferred_element_type=jnp.float32)
        if c < 3:
            q_ref[:, c * cw:(c + 1) * cw] = (r * scale).astype(jnp.bfloat16)
        elif c < 5:
            kv_ref[:, (c - 3) * cw:(c - 2) * cw] = r.astype(jnp.bfloat16)
        else:
            n1_per_tile = ROW_TILE // DFT_N2
            u3d = u_ref.reshape(N_FOURIER_GROUPS, ROW_TILE, FOURIER_GROUP_DIM)
            for g in range(N_FOURIER_GROUPS):
                for n1l in range(n1_per_tile):
                    u3d[g, pl.ds(n1l, DFT_N2, stride=n1_per_tile), :] = r[
                        n1l * DFT_N2:(n1l + 1) * DFT_N2,
                        g * FOURIER_GROUP_DIM:(g + 1) * FOURIER_GROUP_DIM]


def _in_proj(x, g, w):
    n_tiles = SEQ // ROW_TILE
    return pl.pallas_call(
        _in_proj_kernel,
        grid=(n_tiles,),
        in_specs=[
            pl.BlockSpec((ROW_TILE, D_MODEL), lambda i: (i, 0)),
            pl.BlockSpec((1, D_MODEL), lambda i: (0, 0)),
            pl.BlockSpec((D_MODEL, IN_WIDTH), lambda i: (0, 0)),
        ],
        out_specs=[
            pl.BlockSpec((ROW_TILE, ATTN_WIDTH), lambda i: (i, 0)),
            pl.BlockSpec((ROW_TILE, 2 * KV_WIDTH), lambda i: (i, 0)),
            pl.BlockSpec((N_FOURIER_GROUPS, DFT_N2, ROW_TILE // DFT_N2, FOURIER_GROUP_DIM),
                         lambda i: (0, 0, i, 0)),
        ],
        out_shape=[
            jax.ShapeDtypeStruct((SEQ, ATTN_WIDTH), jnp.bfloat16),
            jax.ShapeDtypeStruct((SEQ, 2 * KV_WIDTH), jnp.bfloat16),
            jax.ShapeDtypeStruct((N_FOURIER_GROUPS, DFT_N2, DFT_N1, FOURIER_GROUP_DIM), jnp.float32),
        ],
        compiler_params=_params("arbitrary"),
        name="in_proj",
    )(x, g, w)


def _attention_kernel(sink_ref, q_ref, kvp_ref, kvc_ref, kvn_ref, g_ref, o_ref, acc_ref):
    i = pl.program_id(0)
    slopes = _alibi_slopes(N_Q_HEADS)
    qi = lax.broadcasted_iota(jnp.int32, (BLOCK, 3 * BLOCK), 0)
    kj = lax.broadcasted_iota(jnp.int32, (BLOCK, 3 * BLOCK), 1)
    arel = jnp.abs(kj - BLOCK - qi)
    s_abs = i * BLOCK - BLOCK + kj
    valid = (arel <= WINDOW) & (s_abs >= 0) & (s_abs < SEQ)
    base = jnp.where(valid, -arel.astype(jnp.float32), NEG_INF)

    for kh in range(N_KV_HEADS):
        ksl = slice(kh * HEAD_DIM, (kh + 1) * HEAD_DIM)
        vsl = slice(KV_WIDTH + kh * HEAD_DIM, KV_WIDTH + (kh + 1) * HEAD_DIM)
        kcat = jnp.concatenate([kvp_ref[:, ksl], kvc_ref[:, ksl], kvn_ref[:, ksl]], axis=0)
        vcat = jnp.concatenate([kvp_ref[:, vsl], kvc_ref[:, vsl], kvn_ref[:, vsl]], axis=0)
        qg = jnp.concatenate(
            [q_ref[:, (kh * GQA_GROUP + g) * HEAD_DIM:(kh * GQA_GROUP + g + 1) * HEAD_DIM]
             for g in range(GQA_GROUP)], axis=0)
        s = lax.dot_general(qg, kcat, (((1,), (1,)), ((), ())),
                            preferred_element_type=jnp.float32)
        ps, dens = [], []
        for g in range(GQA_GROUP):
            h = kh * GQA_GROUP + g
            sink_h = sink_ref[h]
            sg = s[g * BLOCK:(g + 1) * BLOCK] + slopes[h] * base
            m = jnp.maximum(jnp.max(sg, axis=-1, keepdims=True), sink_h)
            p = jnp.exp(sg - m)
            dens.append(jnp.sum(p, axis=-1, keepdims=True) + jnp.exp(sink_h - m))
            ps.append(p.astype(jnp.bfloat16))
        o = jnp.dot(jnp.concatenate(ps, axis=0), vcat, preferred_element_type=jnp.float32)
        for g in range(GQA_GROUP):
            h = kh * GQA_GROUP + g
            acc_ref[:, h * HEAD_DIM:(h + 1) * HEAD_DIM] = o[g * BLOCK:(g + 1) * BLOCK] / dens[g]

    a = acc_ref[...]
    o_ref[...] = (a * _rms_scale(a) * g_ref[...]).astype(jnp.bfloat16)


def _attention(sink, q, kv, g):
    nb = SEQ // BLOCK
    return pl.pallas_call(
        _attention_kernel,
        grid_spec=pltpu.PrefetchScalarGridSpec(
            num_scalar_prefetch=1,
            grid=(nb,),
            in_specs=[
                pl.BlockSpec((BLOCK, ATTN_WIDTH), lambda i, s: (i, 0)),
                pl.BlockSpec((BLOCK, 2 * KV_WIDTH), lambda i, s: (jnp.maximum(i - 1, 0), 0)),
                pl.BlockSpec((BLOCK, 2 * KV_WIDTH), lambda i, s: (i, 0)),
                pl.BlockSpec((BLOCK, 2 * KV_WIDTH), lambda i, s: (jnp.minimum(i + 1, nb - 1), 0)),
                pl.BlockSpec((1, ATTN_WIDTH), lambda i, s: (0, 0)),
            ],
            out_specs=pl.BlockSpec((BLOCK, ATTN_WIDTH), lambda i, s: (i, 0)),
            scratch_shapes=[pltpu.VMEM((BLOCK, ATTN_WIDTH), jnp.float32)],
        ),
        out_shape=jax.ShapeDtypeStruct((SEQ, ATTN_WIDTH), jnp.bfloat16),
        compiler_params=_params("arbitrary"),
        name="attention",
    )(sink, q, kv, kv, kv, g)


def _dft_tables():
    n1 = np.arange(DFT_N1, dtype=np.float64)
    n2 = np.arange(DFT_N2, dtype=np.float64)
    ang1 = -2.0 * np.pi * (np.outer(n1, n1)[None] / DFT_N1 + (n2[:, None, None] * n1[None, :, None]) / SEQ)
    f1 = np.concatenate([np.cos(ang1), np.sin(ang1)], axis=1) / np.sqrt(DFT_N1)
    ang2 = -2.0 * np.pi * np.outer(n2, n2) / DFT_N2
    f2r, f2i = np.cos(ang2) / np.sqrt(DFT_N2), np.sin(ang2) / np.sqrt(DFT_N2)
    f2 = np.block([[f2r, -f2i], [f2i, f2r]])
    c = np.arange(FOURIER_GROUP_DIM, dtype=np.float64)
    angc = 2.0 * np.pi * np.outer(c, c) / FOURIER_GROUP_DIM
    cs = np.concatenate([np.cos(angc), np.sin(angc)], axis=0) / np.sqrt(FOURIER_GROUP_DIM)
    return (jnp.asarray(f1, dtype=jnp.float32), jnp.asarray(f2, dtype=jnp.float32),
            jnp.asarray(cs, dtype=jnp.float32))


DFT_BATCH = SUBLANES


def _group_lanes(ref, rows):
    return jnp.concatenate([ref[g, rows, :] for g in range(N_FOURIER_GROUPS)], axis=1)


def _dft_stage1_kernel(u_ref, f1_ref, yr_ref, yi_ref):
    G, C = N_FOURIER_GROUPS, FOURIER_GROUP_DIM
    yr3d = yr_ref.reshape(G, DFT_N1 * DFT_BATCH, C)
    yi3d = yi_ref.reshape(G, DFT_N1 * DFT_BATCH, C)
    for j in range(DFT_BATCH):
        d = _group_lanes(u_ref, slice(j * DFT_N1, (j + 1) * DFT_N1)).astype(jnp.bfloat16)
        y = jnp.dot(f1_ref[j].astype(jnp.bfloat16), d, preferred_element_type=jnp.float32)
        for g in range(G):
            yr3d[g, pl.ds(j, DFT_N1, stride=DFT_BATCH), :] = y[:DFT_N1, g * C:(g + 1) * C]
            yi3d[g, pl.ds(j, DFT_N1, stride=DFT_BATCH), :] = y[DFT_N1:, g * C:(g + 1) * C]


def _dft_stage1(u, f1):
    G, C = N_FOURIER_GROUPS, FOURIER_GROUP_DIM
    n_steps = DFT_N2 // DFT_BATCH
    y_shape = jax.ShapeDtypeStruct((G, DFT_N1, DFT_N2, C), jnp.float32)
    y_spec = pl.BlockSpec((G, DFT_N1, DFT_BATCH, C), lambda s: (0, 0, s, 0))
    return pl.pallas_call(
        _dft_stage1_kernel,
        grid=(n_steps,),
        in_specs=[
            pl.BlockSpec((G, DFT_BATCH * DFT_N1, C), lambda s: (0, s, 0)),
            pl.BlockSpec((DFT_BATCH, 2 * DFT_N1, DFT_N1), lambda s: (s, 0, 0)),
        ],
        out_specs=[y_spec, y_spec],
        out_shape=[y_shape, y_shape],
        compiler_params=_params("arbitrary"),
        name="dft_stage1",
    )(u, f1)


def _dft_stage2_kernel(yr_ref, yi_ref, f2_ref, cs_ref, w_ref, y_ref, x_ref, m_ref):
    G, C = N_FOURIER_GROUPS, FOURIER_GROUP_DIM

    @pl.when(pl.program_id(0) == 0)
    def _():
        for g in range(G):
            m_ref[g] = jnp.dot(cs_ref[...].astype(jnp.bfloat16), w_ref[g].astype(jnp.bfloat16),
                               preferred_element_type=jnp.float32).astype(jnp.bfloat16)

    f2 = f2_ref[...].astype(jnp.bfloat16)
    for j in range(DFT_BATCH):
        rows = slice(j * DFT_N2, (j + 1) * DFT_N2)
        d = jnp.concatenate([_group_lanes(yr_ref, rows), _group_lanes(yi_ref, rows)],
                            axis=0).astype(jnp.bfloat16)
        xc = jnp.dot(f2, d, preferred_element_type=jnp.float32)
        for g in range(G):
            x_ref[g, pl.ds(j, DFT_N2, stride=DFT_BATCH), :] = xc[:DFT_N2, g * C:(g + 1) * C]
            x_ref[G + g, pl.ds(j, DFT_N2, stride=DFT_BATCH), :] = xc[DFT_N2:, g * C:(g + 1) * C]

    for g in range(G):
        m = m_ref[g]
        y_ref[g] = (
            jnp.dot(x_ref[g].astype(jnp.bfloat16), m[:C], preferred_element_type=jnp.float32)
            + jnp.dot(x_ref[G + g].astype(jnp.bfloat16), m[C:], preferred_element_type=jnp.float32)
        ).reshape(DFT_N2, DFT_BATCH, C)


def _dft_stage2(yr, yi, f2, cs, w_fourier):
    G, C = N_FOURIER_GROUPS, FOURIER_GROUP_DIM
    n_steps = DFT_N1 // DFT_BATCH
    y_spec = pl.BlockSpec((G, DFT_BATCH * DFT_N2, C), lambda s: (0, s, 0))
    return pl.pallas_call(
        _dft_stage2_kernel,
        grid=(n_steps,),
        in_specs=[
            y_spec, y_spec,
            pl.BlockSpec((2 * DFT_N2, 2 * DFT_N2), lambda s: (0, 0)),
            pl.BlockSpec((2 * C, C), lambda s: (0, 0)),
            pl.BlockSpec((G, C, C), lambda s: (0, 0, 0)),
        ],
        out_specs=pl.BlockSpec((G, DFT_N2, DFT_BATCH, C), lambda s: (0, 0, s, 0)),
        out_shape=jax.ShapeDtypeStruct((G, DFT_N2, DFT_N1, C), jnp.float32),
        scratch_shapes=[
            pltpu.VMEM((2 * G, DFT_BATCH * DFT_N2, C), jnp.float32),
            pltpu.VMEM((G, 2 * C, C), jnp.bfloat16),
        ],
        compiler_params=_params("arbitrary"),
        name="dft_stage2",
    )(yr, yi, f2, cs, w_fourier)


def _fourier(u_sw, w_fourier):
    G, C = N_FOURIER_GROUPS, FOURIER_GROUP_DIM
    f1, f2, cs = _dft_tables()
    yr, yi = _dft_stage1(u_sw.reshape(G, SEQ, C), f1)
    y = _dft_stage2(yr.reshape(G, SEQ, C), yi.reshape(G, SEQ, C), f2, cs, w_fourier)
    return y.reshape(G, SEQ, C)


def _out_proj_kernel(a_ref, y_ref, gf_ref, w_ref, x_ref, g2_ref, x1_ref, h2_ref):
    y = _group_lanes(y_ref, slice(None))
    f = (y * _rms_scale(y) * gf_ref[...]).astype(jnp.bfloat16)
    a = a_ref[...]
    cw = 512
    ssq = jnp.zeros((ROW_TILE, 1), jnp.float32)
    for c in range(D_MODEL // cw):
        cs = slice(c * cw, (c + 1) * cw)
        r = (jnp.dot(a, w_ref[:ATTN_WIDTH, cs], preferred_element_type=jnp.float32)
             + jnp.dot(f, w_ref[ATTN_WIDTH:, cs], preferred_element_type=jnp.float32))
        x1 = x_ref[:, cs] + r
        x1_ref[:, cs] = x1
        ssq = ssq + jnp.sum(x1 * x1, axis=-1, keepdims=True)
    rs = lax.rsqrt(ssq * (1.0 / D_MODEL) + EPS)
    h2_ref[...] = (x1_ref[...] * rs * g2_ref[...]).astype(jnp.bfloat16)


def _out_proj(a, y, gf, w, x, g2):
    n_tiles = SEQ // ROW_TILE
    return pl.pallas_call(
        _out_proj_kernel,
        grid=(n_tiles,),
        in_specs=[
            pl.BlockSpec((ROW_TILE, ATTN_WIDTH), lambda i: (i, 0)),
            pl.BlockSpec((N_FOURIER_GROUPS, ROW_TILE, FOURIER_GROUP_DIM), lambda i: (0, i, 0)),
            pl.BlockSpec((1, FOURIER_WIDTH), lambda i: (0, 0)),
            pl.BlockSpec((D_MODEL, D_MODEL), lambda i: (0, 0)),
            pl.BlockSpec((ROW_TILE, D_MODEL), lambda i: (i, 0)),
            pl.BlockSpec((1, D_MODEL), lambda i: (0, 0)),
        ],
        out_specs=[
            pl.BlockSpec((ROW_TILE, D_MODEL), lambda i: (i, 0)),
            pl.BlockSpec((ROW_TILE, D_MODEL), lambda i: (i, 0)),
        ],
        out_shape=[
            jax.ShapeDtypeStruct((SEQ, D_MODEL), jnp.float32),
            jax.ShapeDtypeStruct((SEQ, D_MODEL), jnp.bfloat16),
        ],
        compiler_params=_params("arbitrary"),
        name="out_proj",
    )(a, y, gf, w, x, g2)


def _gelu(v):
    return 0.5 * v * (1.0 + lax.erf(v * (2.0 ** -0.5)))


def _ffn_step(hx_ref, wg_ref, wv_ref, up_refs, dww_ref, dwb_ref, wd_ref, down_refs, o_ref):
    H, T, B = HALO_ROWS, FF_ROW_TILE, FF_ROW_BLOCK
    n_blocks = T // B
    if up_refs is not None:
        wg = wg_ref[...]
        wv = wv_ref[...]
        gate_up, val_up = up_refs
        gate_edges = [(T + 2 * H) * b // n_blocks // H * H for b in range(n_blocks + 1)]
    gate_dn, val_dn = down_refs
    w = dww_ref[...]
    wd = wd_ref[...]
    for b in range(n_blocks):
        if up_refs is not None:
            g0, g1 = gate_edges[b], gate_edges[b + 1]
            gate_up[g0:g1] = jnp.dot(hx_ref[g0:g1], wg, preferred_element_type=jnp.float32)
            val_up[b * B:(b + 1) * B] = jnp.dot(hx_ref[H + b * B:H + (b + 1) * B], wv,
                                                preferred_element_type=jnp.float32)
        r0, r1 = b * B, (b + 1) * B
        conv = (w[0:1] * gate_dn[H + r0 - 1:H + r1 - 1] + w[1:2] * gate_dn[H + r0:H + r1]
                + w[2:3] * gate_dn[H + r0 + 1:H + r1 + 1] + dwb_ref[...])
        act = (_gelu(conv) * val_dn[r0:r1]).astype(jnp.bfloat16)
        o_ref[r0:r1] += jnp.dot(act, wd, preferred_element_type=jnp.float32)


def _conv_ffn_kernel(h_ref, hp_ref, hn_ref, wg_ref, wv_ref, dww_ref, dwb_ref, wd_ref,
                     x1_ref, gf_ref, o_ref, hx_ref, gate0_ref, val0_ref, gate1_ref, val1_ref):
    i = pl.program_id(0)
    c = pl.program_id(1)
    T = FF_ROW_TILE
    H = HALO_ROWS

    @pl.when(c == 0)
    def _():
        hx_ref[0:H] = jnp.where(i == 0, jnp.zeros_like(hp_ref), hp_ref[...])
        hx_ref[H:H + T] = h_ref[...]
        hx_ref[H + T:] = jnp.where(i == pl.num_programs(0) - 1, jnp.zeros_like(hn_ref), hn_ref[...])
        o_ref[...] = jnp.zeros_like(o_ref)
        gate1_ref[...] = jnp.zeros_like(gate1_ref)
        val1_ref[...] = jnp.zeros_like(val1_ref)

    for k in range(D_MODEL // X1_PANEL):
        @pl.when(c == k)
        def _(k=k):
            o_ref[:, k * X1_PANEL:(k + 1) * X1_PANEL] += x1_ref[...]

    slot0, slot1 = (gate0_ref, val0_ref), (gate1_ref, val1_ref)

    @pl.when((c < N_FF_CHUNKS) & (c % 2 == 0))
    def _():
        _ffn_step(hx_ref, wg_ref, wv_ref, slot0, dww_ref, dwb_ref, wd_ref, slot1, o_ref)

    @pl.when((c < N_FF_CHUNKS) & (c % 2 == 1))
    def _():
        _ffn_step(hx_ref, wg_ref, wv_ref, slot1, dww_ref, dwb_ref, wd_ref, slot0, o_ref)

    @pl.when(c == N_FF_CHUNKS)
    def _():
        last = slot0 if (N_FF_CHUNKS - 1) % 2 == 0 else slot1
        _ffn_step(hx_ref, wg_ref, wv_ref, None, dww_ref, dwb_ref, wd_ref, last, o_ref)
        o = o_ref[...]
        o_ref[...] = o * _rms_scale(o) * gf_ref[...]


def _conv_ffn(h2, wg, wv, dww, dwb, wd, x1, gf):
    n_tiles = SEQ // FF_ROW_TILE
    halo_per_tile = FF_ROW_TILE // HALO_ROWS
    n_halo = SEQ // HALO_ROWS
    T, H = FF_ROW_TILE, HALO_ROWS

    def up_chunk(c):
        return jnp.minimum(c, N_FF_CHUNKS - 1)

    def down_chunk(c):
        return jnp.maximum(c - 1, 0)

    return pl.pallas_call(
        _conv_ffn_kernel,
        grid=(n_tiles, N_FF_CHUNKS + 1),
        in_specs=[
            pl.BlockSpec((T, D_MODEL), lambda i, c: (i, 0)),
            pl.BlockSpec((H, D_MODEL), lambda i, c: (jnp.maximum(i * halo_per_tile - 1, 0), 0)),
            pl.BlockSpec((H, D_MODEL),
                         lambda i, c: (jnp.minimum((i + 1) * halo_per_tile, n_halo - 1), 0)),
            pl.BlockSpec((None, D_MODEL, FF_CHUNK), lambda i, c: (up_chunk(c), 0, 0)),
            pl.BlockSpec((None, D_MODEL, FF_CHUNK), lambda i, c: (up_chunk(c), 0, 0)),
            pl.BlockSpec((3, FF_CHUNK), lambda i, c: (0, down_chunk(c))),
            pl.BlockSpec((1, FF_CHUNK), lambda i, c: (0, down_chunk(c))),
            pl.BlockSpec((FF_CHUNK, D_MODEL), lambda i, c: (down_chunk(c), 0)),
            pl.BlockSpec((T, X1_PANEL),
                         lambda i, c: (i, jnp.minimum(c, D_MODEL // X1_PANEL - 1))),
            pl.BlockSpec((1, D_MODEL), lambda i, c: (0, 0)),
        ],
        out_specs=pl.BlockSpec((T, D_MODEL), lambda i, c: (i, 0)),
        out_shape=jax.ShapeDtypeStruct((SEQ, D_MODEL), jnp.float32),
        scratch_shapes=[
            pltpu.VMEM((T + 2 * H, D_MODEL), jnp.bfloat16),
            pltpu.VMEM((T + 2 * H, FF_CHUNK), jnp.float32), pltpu.VMEM((T, FF_CHUNK), jnp.float32),
            pltpu.VMEM((T + 2 * H, FF_CHUNK), jnp.float32), pltpu.VMEM((T, FF_CHUNK), jnp.float32),
        ],
        compiler_params=_params("arbitrary", "arbitrary"),
        name="conv_ffn",
    )(h2, h2, h2, wg, wv, dww, dwb, wd, x1, gf)


def kernel(x, norm1_g, w_in, sink, w_fourier, attn_out_g, fourier_out_g, w_out, norm2_g,
           w_up, dw_w, dw_b, w_down, normf_g):
    bf16 = jnp.bfloat16
    x2 = x.reshape(SEQ, D_MODEL)
    pad = D_FF_PAD - D_FF

    def chunk_major(w_cols):
        w = jnp.pad(w_cols.astype(bf16), ((0, 0), (0, pad)))
        return w.reshape(D_MODEL, N_FF_CHUNKS, FF_CHUNK).transpose(1, 0, 2)

    wg, wv = chunk_major(w_up[:, :D_FF]), chunk_major(w_up[:, D_FF:])
    wd = jnp.pad(w_down.astype(bf16), ((0, pad), (0, 0)))
    dww = jnp.pad(dw_w.reshape(3, D_FF), ((0, 0), (0, pad)))
    dwb = jnp.pad(dw_b.reshape(1, D_FF), ((0, 0), (0, pad)))
    q, kv, u_sw = _in_proj(x2, norm1_g.reshape(1, D_MODEL), w_in.astype(bf16))
    a = _attention(sink, q, kv, attn_out_g.reshape(1, ATTN_WIDTH))
    y = _fourier(u_sw, w_fourier)
    x1, h2 = _out_proj(a, y, fourier_out_g.reshape(1, FOURIER_WIDTH), w_out.astype(bf16),
                       x2, norm2_g.reshape(1, D_MODEL))
    out = _conv_ffn(h2, wg, wv, dww, dwb, wd, x1, normf_g.reshape(1, D_MODEL))
    return out.reshape(x.shape)
```

```python
import math

import numpy as np
import jax
import jax.numpy as jnp
from jax import lax
from jax.experimental import pallas as pl
from jax.experimental.pallas import tpu as pltpu

D_MODEL = 2048
SEQ = 8192
HEAD_DIM = 128
FOURIER_WIDTH = 512
ATTN_WIDTH = 1536
N_Q_HEADS = 12
GQA_GROUP = 3
N_KV_HEADS = 4
KV_WIDTH = 512
FOURIER_GROUP_DIM = 128
N_FOURIER_GROUPS = 4
IN_WIDTH = 3072
WINDOW = 128
BLOCK = 128
D_FF = 5504
EPS = 1e-6
NEG_INF = -1e30
LOG2E = math.log2(math.e)

VMEM_LIMIT_BYTES = 60 * 1024 * 1024
SUBLANES = 8

DFT_N1 = 128
DFT_N2 = SEQ // DFT_N1

ROW_TILE = 512
FF_ROW_TILE = 1024
FF_CHUNK = 256
FF_ROW_BLOCK = 256
X1_PANEL = 256
N_FF_CHUNKS = -(-D_FF // FF_CHUNK)
HALO_ROWS = 16


def _alibi_slopes(n_heads):
    def pow2_slopes(n):
        start = 2.0 ** (-8.0 / n)
        return [start ** (i + 1) for i in range(n)]
    if math.log2(n_heads).is_integer():
        s = pow2_slopes(n_heads)
    else:
        closest = 2 ** int(math.floor(math.log2(n_heads)))
        s = pow2_slopes(closest) + pow2_slopes(2 * closest)[0::2][: n_heads - closest]
    return [float(np.float32(v)) for v in s]


def _rms_scale(v):
    return lax.rsqrt(jnp.mean(v * v, axis=-1, keepdims=True) + EPS)


def _params(*semantics):
    return pltpu.CompilerParams(dimension_semantics=semantics, vmem_limit_bytes=VMEM_LIMIT_BYTES)


def _in_proj_kernel(x_ref, g_ref, w_ref, q_ref, kv_ref, u_ref):
    x = x_ref[...]
    h = (x * _rms_scale(x) * g_ref[...]).astype(jnp.bfloat16)
    scale = HEAD_DIM ** -0.5 * LOG2E
    cw = 512
    for c in range(IN_WIDTH // cw):
        r = jnp.dot(h, w_ref[:, c * cw:(c + 1) * cw].astype(jnp.bfloat16),
                    preferred_element_type=jnp.float32)
        if c < 3:
            q_ref[:, c * cw:(c + 1) * cw] = (r * scale).astype(jnp.bfloat16)
        elif c < 5:
            kv_ref[:, (c - 3) * cw:(c - 2) * cw] = r.astype(jnp.bfloat16)
        else:
            n1_per_tile = ROW_TILE // DFT_N2
            u3d = u_ref.reshape(N_FOURIER_GROUPS, ROW_TILE, FOURIER_GROUP_DIM)
            for g in range(N_FOURIER_GROUPS):
                for n1l in range(n1_per_tile):
                    u3d[g, pl.ds(n1l, DFT_N2, stride=n1_per_tile), :] = r[
                        n1l * DFT_N2:(n1l + 1) * DFT_N2,
                        g * FOURIER_GROUP_DIM:(g + 1) * FOURIER_GROUP_DIM]


def _in_proj(x, g, w):
    n_tiles = SEQ // ROW_TILE
    return pl.pallas_call(
        _in_proj_kernel,
        grid=(n_tiles,),
        in_specs=[
            pl.BlockSpec((ROW_TILE, D_MODEL), lambda i: (i, 0)),
            pl.BlockSpec((1, D_MODEL), lambda i: (0, 0)),
            pl.BlockSpec((D_MODEL, IN_WIDTH), lambda i: (0, 0), pipeline_mode=pl.Buffered(1)),
        ],
        out_specs=[
            pl.BlockSpec((ROW_TILE, ATTN_WIDTH), lambda i: (i, 0)),
            pl.BlockSpec((ROW_TILE, 2 * KV_WIDTH), lambda i: (i, 0)),
            pl.BlockSpec((N_FOURIER_GROUPS, DFT_N2, ROW_TILE // DFT_N2, FOURIER_GROUP_DIM),
                         lambda i: (0, 0, i, 0)),
        ],
        out_shape=[
            jax.ShapeDtypeStruct((SEQ, ATTN_WIDTH), jnp.bfloat16),
            jax.ShapeDtypeStruct((SEQ, 2 * KV_WIDTH), jnp.bfloat16),
            jax.ShapeDtypeStruct((N_FOURIER_GROUPS, DFT_N2, DFT_N1, FOURIER_GROUP_DIM), jnp.float32),
        ],
        compiler_params=_params("arbitrary"),
        name="in_proj",
    )(x, g, w)


def _attention_kernel(sink_ref, q_ref, kvp_ref, kvc_ref, kvn_ref, g_ref, o_ref, acc_ref):
    i = pl.program_id(0)
    slopes = _alibi_slopes(N_Q_HEADS)
    qi = lax.broadcasted_iota(jnp.int32, (BLOCK, 3 * BLOCK), 0)
    kj = lax.broadcasted_iota(jnp.int32, (BLOCK, 3 * BLOCK), 1)
    arel = jnp.abs(kj - BLOCK - qi)
    s_abs = i * BLOCK - BLOCK + kj
    valid = (arel <= WINDOW) & (s_abs >= 0) & (s_abs < SEQ)
    base = jnp.where(valid, -arel.astype(jnp.float32), NEG_INF)

    def band(col0):
        sl = slice(col0, col0 + HEAD_DIM)
        return jnp.concatenate([kvp_ref[:, sl], kvc_ref[:, sl], kvn_ref[:, sl]], axis=0)

    scores = []
    for kh in range(N_KV_HEADS):
        qg = jnp.concatenate(
            [q_ref[:, (kh * GQA_GROUP + g) * HEAD_DIM:(kh * GQA_GROUP + g + 1) * HEAD_DIM]
             for g in range(GQA_GROUP)], axis=0)
        scores.append(lax.dot_general(qg, band(kh * HEAD_DIM), (((1,), (1,)), ((), ())),
                                      preferred_element_type=jnp.float32))

    for kh in range(N_KV_HEADS):
        s = scores[kh]
        vcat = band(KV_WIDTH + kh * HEAD_DIM)
        ps, dens = [], []
        for g in range(GQA_GROUP):
            h = kh * GQA_GROUP + g
            sink_h = sink_ref[h] * LOG2E
            sg = s[g * BLOCK:(g + 1) * BLOCK] + (slopes[h] * LOG2E) * base
            m = jnp.maximum(jnp.max(sg, axis=-1, keepdims=True), sink_h)
            p = jnp.exp2(sg - m)
            dens.append(jnp.sum(p, axis=-1, keepdims=True) + jnp.exp2(sink_h - m))
            ps.append(p.astype(jnp.bfloat16))
        o = jnp.dot(jnp.concatenate(ps, axis=0), vcat, preferred_element_type=jnp.float32)
        for g in range(GQA_GROUP):
            h = kh * GQA_GROUP + g
            acc_ref[:, h * HEAD_DIM:(h + 1) * HEAD_DIM] = o[g * BLOCK:(g + 1) * BLOCK] / dens[g]

    a = acc_ref[...]
    o_ref[...] = (a * _rms_scale(a) * g_ref[...]).astype(jnp.bfloat16)


def _attention(sink, q, kv, g):
    nb = SEQ // BLOCK
    return pl.pallas_call(
        _attention_kernel,
        grid_spec=pltpu.PrefetchScalarGridSpec(
            num_scalar_prefetch=1,
            grid=(nb,),
            in_specs=[
                pl.BlockSpec((BLOCK, ATTN_WIDTH), lambda i, s: (i, 0)),
                pl.BlockSpec((BLOCK, 2 * KV_WIDTH), lambda i, s: (jnp.maximum(i - 1, 0), 0)),
                pl.BlockSpec((BLOCK, 2 * KV_WIDTH), lambda i, s: (i, 0)),
                pl.BlockSpec((BLOCK, 2 * KV_WIDTH), lambda i, s: (jnp.minimum(i + 1, nb - 1), 0)),
                pl.BlockSpec((1, ATTN_WIDTH), lambda i, s: (0, 0)),
            ],
            out_specs=pl.BlockSpec((BLOCK, ATTN_WIDTH), lambda i, s: (i, 0)),
            scratch_shapes=[pltpu.VMEM((BLOCK, ATTN_WIDTH), jnp.float32)],
        ),
        out_shape=jax.ShapeDtypeStruct((SEQ, ATTN_WIDTH), jnp.bfloat16),
        compiler_params=_params("arbitrary"),
        name="attention",
    )(sink, q, kv, kv, kv, g)


def _dft_tables():
    n1 = np.arange(DFT_N1, dtype=np.float64)
    n2 = np.arange(DFT_N2, dtype=np.float64)
    ang1 = -2.0 * np.pi * (np.outer(n1, n1)[None] / DFT_N1 + (n2[:, None, None] * n1[None, :, None]) / SEQ)
    f1 = np.concatenate([np.cos(ang1), np.sin(ang1)], axis=1) / np.sqrt(DFT_N1)
    ang2 = -2.0 * np.pi * np.outer(n2, n2) / DFT_N2
    f2r, f2i = np.cos(ang2) / np.sqrt(DFT_N2), np.sin(ang2) / np.sqrt(DFT_N2)
    f2 = np.block([[f2r, -f2i], [f2i, f2r]])
    c = np.arange(FOURIER_GROUP_DIM, dtype=np.float64)
    angc = 2.0 * np.pi * np.outer(c, c) / FOURIER_GROUP_DIM
    cs = np.concatenate([np.cos(angc), np.sin(angc)], axis=0) / np.sqrt(FOURIER_GROUP_DIM)
    return (jnp.asarray(f1, dtype=jnp.float32), jnp.asarray(f2, dtype=jnp.float32),
            jnp.asarray(cs, dtype=jnp.float32))


DFT_BATCH = SUBLANES


def _group_lanes(ref, rows):
    return jnp.concatenate([ref[g, rows, :] for g in range(N_FOURIER_GROUPS)], axis=1)


def _dft_stage1_kernel(u_ref, f1_ref, yr_ref, yi_ref):
    G, C = N_FOURIER_GROUPS, FOURIER_GROUP_DIM
    yr3d = yr_ref.reshape(G, DFT_N1 * DFT_BATCH, C)
    yi3d = yi_ref.reshape(G, DFT_N1 * DFT_BATCH, C)
    for j in range(DFT_BATCH):
        d = _group_lanes(u_ref, slice(j * DFT_N1, (j + 1) * DFT_N1)).astype(jnp.bfloat16)
        y = jnp.dot(f1_ref[j].astype(jnp.bfloat16), d, preferred_element_type=jnp.float32)
        for g in range(G):
            yr3d[g, pl.ds(j, DFT_N1, stride=DFT_BATCH), :] = y[:DFT_N1, g * C:(g + 1) * C]
            yi3d[g, pl.ds(j, DFT_N1, stride=DFT_BATCH), :] = y[DFT_N1:, g * C:(g + 1) * C]


def _dft_stage1(u, f1):
    G, C = N_FOURIER_GROUPS, FOURIER_GROUP_DIM
    n_steps = DFT_N2 // DFT_BATCH
    y_shape = jax.ShapeDtypeStruct((G, DFT_N1, DFT_N2, C), jnp.float32)
    y_spec = pl.BlockSpec((G, DFT_N1, DFT_BATCH, C), lambda s: (0, 0, s, 0))
    return pl.pallas_call(
        _dft_stage1_kernel,
        grid=(n_steps,),
        in_specs=[
            pl.BlockSpec((G, DFT_BATCH * DFT_N1, C), lambda s: (0, s, 0)),
            pl.BlockSpec((DFT_BATCH, 2 * DFT_N1, DFT_N1), lambda s: (s, 0, 0)),
        ],
        out_specs=[y_spec, y_spec],
        out_shape=[y_shape, y_shape],
        compiler_params=_params("arbitrary"),
        name="dft_stage1",
    )(u, f1)


def _dft_stage2_kernel(yr_ref, yi_ref, f2_ref, cs_ref, w_ref, y_ref, x_ref, m_ref):
    G, C = N_FOURIER_GROUPS, FOURIER_GROUP_DIM

    @pl.when(pl.program_id(0) == 0)
    def _():
        for g in range(G):
            m_ref[g] = jnp.dot(cs_ref[...].astype(jnp.bfloat16), w_ref[g].astype(jnp.bfloat16),
                               preferred_element_type=jnp.float32).astype(jnp.bfloat16)

    f2 = f2_ref[...].astype(jnp.bfloat16)
    for j in range(DFT_BATCH):
        rows = slice(j * DFT_N2, (j + 1) * DFT_N2)
        d = jnp.concatenate([_group_lanes(yr_ref, rows), _group_lanes(yi_ref, rows)],
                            axis=0).astype(jnp.bfloat16)
        xc = jnp.dot(f2, d, preferred_element_type=jnp.float32)
        for g in range(G):
            x_ref[g, pl.ds(j, DFT_N2, stride=DFT_BATCH), :] = xc[:DFT_N2, g * C:(g + 1) * C]
            x_ref[G + g, pl.ds(j, DFT_N2, stride=DFT_BATCH), :] = xc[DFT_N2:, g * C:(g + 1) * C]

    for g in range(G):
        m = m_ref[g]
        y_ref[g] = (
            jnp.dot(x_ref[g].astype(jnp.bfloat16), m[:C], preferred_element_type=jnp.float32)
            + jnp.dot(x_ref[G + g].astype(jnp.bfloat16), m[C:], preferred_element_type=jnp.float32)
        ).reshape(DFT_N2, DFT_BATCH, C)


def _dft_stage2(yr, yi, f2, cs, w_fourier):
    G, C = N_FOURIER_GROUPS, FOURIER_GROUP_DIM
    n_steps = DFT_N1 // DFT_BATCH
    y_spec = pl.BlockSpec((G, DFT_BATCH * DFT_N2, C), lambda s: (0, s, 0))
    return pl.pallas_call(
        _dft_stage2_kernel,
        grid=(n_steps,),
        in_specs=[
            y_spec, y_spec,
            pl.BlockSpec((2 * DFT_N2, 2 * DFT_N2), lambda s: (0, 0)),
            pl.BlockSpec((2 * C, C), lambda s: (0, 0)),
            pl.BlockSpec((G, C, C), lambda s: (0, 0, 0)),
        ],
        out_specs=pl.BlockSpec((G, DFT_N2, DFT_BATCH, C), lambda s: (0, 0, s, 0)),
        out_shape=jax.ShapeDtypeStruct((G, DFT_N2, DFT_N1, C), jnp.float32),
        scratch_shapes=[
            pltpu.VMEM((2 * G, DFT_BATCH * DFT_N2, C), jnp.float32),
            pltpu.VMEM((G, 2 * C, C), jnp.bfloat16),
        ],
        compiler_params=_params("arbitrary"),
        name="dft_stage2",
    )(yr, yi, f2, cs, w_fourier)


def _fourier(u_sw, w_fourier):
    G, C = N_FOURIER_GROUPS, FOURIER_GROUP_DIM
    f1, f2, cs = _dft_tables()
    yr, yi = _dft_stage1(u_sw.reshape(G, SEQ, C), f1)
    y = _dft_stage2(yr.reshape(G, SEQ, C), yi.reshape(G, SEQ, C), f2, cs, w_fourier)
    return y.reshape(G, SEQ, C)


def _out_proj_kernel(a_ref, y_ref, gf_ref, w_ref, x_ref, g2_ref, x1_ref, h2_ref):
    y = _group_lanes(y_ref, slice(None))
    f = (y * _rms_scale(y) * gf_ref[...]).astype(jnp.bfloat16)
    a = a_ref[...]
    cw = 512
    ssq = jnp.zeros((ROW_TILE, 1), jnp.float32)
    for c in range(D_MODEL // cw):
        cs = slice(c * cw, (c + 1) * cw)
        r = (jnp.dot(a, w_ref[:ATTN_WIDTH, cs].astype(jnp.bfloat16),
                     preferred_element_type=jnp.float32)
             + jnp.dot(f, w_ref[ATTN_WIDTH:, cs].astype(jnp.bfloat16),
                       preferred_element_type=jnp.float32))
        x1 = x_ref[:, cs] + r
        x1_ref[:, cs] = x1
        ssq = ssq + jnp.sum(x1 * x1, axis=-1, keepdims=True)
    rs = lax.rsqrt(ssq * (1.0 / D_MODEL) + EPS)
    h2_ref[...] = (x1_ref[...] * rs * g2_ref[...]).astype(jnp.bfloat16)


def _out_proj(a, y, gf, w, x, g2):
    n_tiles = SEQ // ROW_TILE
    return pl.pallas_call(
        _out_proj_kernel,
        grid=(n_tiles,),
        in_specs=[
            pl.BlockSpec((ROW_TILE, ATTN_WIDTH), lambda i: (i, 0)),
            pl.BlockSpec((N_FOURIER_GROUPS, ROW_TILE, FOURIER_GROUP_DIM), lambda i: (0, i, 0)),
            pl.BlockSpec((1, FOURIER_WIDTH), lambda i: (0, 0)),
            pl.BlockSpec((D_MODEL, D_MODEL), lambda i: (0, 0), pipeline_mode=pl.Buffered(1)),
            pl.BlockSpec((ROW_TILE, D_MODEL), lambda i: (i, 0)),
            pl.BlockSpec((1, D_MODEL), lambda i: (0, 0)),
        ],
        out_specs=[
            pl.BlockSpec((ROW_TILE, D_MODEL), lambda i: (i, 0)),
            pl.BlockSpec((ROW_TILE, D_MODEL), lambda i: (i, 0)),
        ],
        out_shape=[
            jax.ShapeDtypeStruct((SEQ, D_MODEL), jnp.float32),
            jax.ShapeDtypeStruct((SEQ, D_MODEL), jnp.bfloat16),
        ],
        compiler_params=_params("arbitrary"),
        name="out_proj",
    )(a, y, gf, w, x, g2)


def _gelu(v):
    return 0.5 * v * (1.0 + lax.erf(v * (2.0 ** -0.5)))


def _ff_chunk_start(c):
    return pl.multiple_of(jnp.minimum(c * FF_CHUNK, D_FF - FF_CHUNK), 128)


def _ffn_step(chunk, hx_ref, wg_ref, wv_ref, up_refs, dww_ref, dwb_ref, wd_ref, down_refs, o_ref):
    H, T, B = HALO_ROWS, FF_ROW_TILE, FF_ROW_BLOCK
    n_blocks = T // B
    if up_refs is not None:
        wg = wg_ref[...].astype(jnp.bfloat16)
        wv = wv_ref[...].astype(jnp.bfloat16)
        gate_up, val_up = up_refs
        gate_edges = [(T + 2 * H) * b // n_blocks // H * H for b in range(n_blocks + 1)]
    gate_dn, val_dn = down_refs
    w = dww_ref[...]
    wd = wd_ref[...].astype(jnp.bfloat16)
    overlap = chunk * FF_CHUNK - _ff_chunk_start(chunk)
    keep = lax.broadcasted_iota(jnp.int32, (1, FF_CHUNK), 1) >= overlap
    for b in range(n_blocks):
        if up_refs is not None:
            g0, g1 = gate_edges[b], gate_edges[b + 1]
            gate_up[g0:g1] = jnp.dot(hx_ref[g0:g1], wg, preferred_element_type=jnp.float32)
            val_up[b * B:(b + 1) * B] = jnp.dot(hx_ref[H + b * B:H + (b + 1) * B], wv,
                                                preferred_element_type=jnp.float32)
        r0, r1 = b * B, (b + 1) * B
        conv = (w[0:1] * gate_dn[H + r0 - 1:H + r1 - 1] + w[1:2] * gate_dn[H + r0:H + r1]
                + w[2:3] * gate_dn[H + r0 + 1:H + r1 + 1] + dwb_ref[...])
        act = jnp.where(keep, _gelu(conv) * val_dn[r0:r1], 0.0).astype(jnp.bfloat16)
        o_ref[r0:r1] += jnp.dot(act, wd, preferred_element_type=jnp.float32)


def _conv_ffn_kernel(h_ref, hp_ref, hn_ref, wg_ref, wv_ref, dww_ref, dwb_ref, wd_ref,
                     x1_ref, gf_ref, o_ref, hx_ref, gate0_ref, val0_ref, gate1_ref, val1_ref):
    i = pl.program_id(0)
    c = pl.program_id(1)
    T = FF_ROW_TILE
    H = HALO_ROWS
    prev = jnp.maximum(c - 1, 0)

    @pl.when(c == 0)
    def _():
        hx_ref[0:H] = jnp.where(i == 0, jnp.zeros_like(hp_ref), hp_ref[...])
        hx_ref[H:H + T] = h_ref[...]
        hx_ref[H + T:] = jnp.where(i == pl.num_programs(0) - 1, jnp.zeros_like(hn_ref), hn_ref[...])
        o_ref[...] = jnp.zeros_like(o_ref)
        gate1_ref[...] = jnp.zeros_like(gate1_ref)
        val1_ref[...] = jnp.zeros_like(val1_ref)

    for k in range(D_MODEL // X1_PANEL):
        @pl.when(c == k)
        def _(k=k):
            o_ref[:, k * X1_PANEL:(k + 1) * X1_PANEL] += x1_ref[...]

    slot0, slot1 = (gate0_ref, val0_ref), (gate1_ref, val1_ref)

    @pl.when((c < N_FF_CHUNKS) & (c % 2 == 0))
    def _():
        _ffn_step(prev, hx_ref, wg_ref, wv_ref, slot0, dww_ref, dwb_ref, wd_ref, slot1, o_ref)

    @pl.when((c < N_FF_CHUNKS) & (c % 2 == 1))
    def _():
        _ffn_step(prev, hx_ref, wg_ref, wv_ref, slot1, dww_ref, dwb_ref, wd_ref, slot0, o_ref)

    @pl.when(c == N_FF_CHUNKS)
    def _():
        last = slot0 if (N_FF_CHUNKS - 1) % 2 == 0 else slot1
        _ffn_step(prev, hx_ref, wg_ref, wv_ref, None, dww_ref, dwb_ref, wd_ref, last, o_ref)
        o = o_ref[...]
        o_ref[...] = o * _rms_scale(o) * gf_ref[...]


def _conv_ffn(h2, w_up, dww, dwb, w_down, x1, gf):
    n_tiles = SEQ // FF_ROW_TILE
    halo_per_tile = FF_ROW_TILE // HALO_ROWS
    n_halo = SEQ // HALO_ROWS
    E = pl.Element
    T, H = FF_ROW_TILE, HALO_ROWS

    def up_start(c):
        return _ff_chunk_start(jnp.minimum(c, N_FF_CHUNKS - 1))

    def down_start(c):
        return _ff_chunk_start(jnp.maximum(c - 1, 0))

    return pl.pallas_call(
        _conv_ffn_kernel,
        grid=(n_tiles, N_FF_CHUNKS + 1),
        in_specs=[
            pl.BlockSpec((T, D_MODEL), lambda i, c: (i, 0)),
            pl.BlockSpec((H, D_MODEL), lambda i, c: (jnp.maximum(i * halo_per_tile - 1, 0), 0)),
            pl.BlockSpec((H, D_MODEL),
                         lambda i, c: (jnp.minimum((i + 1) * halo_per_tile, n_halo - 1), 0)),
            pl.BlockSpec((E(D_MODEL), E(FF_CHUNK)), lambda i, c: (0, up_start(c))),
            pl.BlockSpec((E(D_MODEL), E(FF_CHUNK)),
                         lambda i, c: (0, pl.multiple_of(D_FF + up_start(c), 128))),
            pl.BlockSpec((E(3), E(FF_CHUNK)), lambda i, c: (0, down_start(c))),
            pl.BlockSpec((E(1), E(FF_CHUNK)), lambda i, c: (0, down_start(c))),
            pl.BlockSpec((E(FF_CHUNK), E(D_MODEL)), lambda i, c: (down_start(c), 0)),
            pl.BlockSpec((T, X1_PANEL),
                         lambda i, c: (i, jnp.minimum(c, D_MODEL // X1_PANEL - 1))),
            pl.BlockSpec((1, D_MODEL), lambda i, c: (0, 0)),
        ],
        out_specs=pl.BlockSpec((T, D_MODEL), lambda i, c: (i, 0)),
        out_shape=jax.ShapeDtypeStruct((SEQ, D_MODEL), jnp.float32),
        scratch_shapes=[
            pltpu.VMEM((T + 2 * H, D_MODEL), jnp.bfloat16),
            pltpu.VMEM((T + 2 * H, FF_CHUNK), jnp.float32), pltpu.VMEM((T, FF_CHUNK), jnp.float32),
            pltpu.VMEM((T + 2 * H, FF_CHUNK), jnp.float32), pltpu.VMEM((T, FF_CHUNK), jnp.float32),
        ],
        compiler_params=_params("arbitrary", "arbitrary"),
        name="conv_ffn",
    )(h2, h2, h2, w_up, w_up, dww, dwb, w_down, x1, gf)


def kernel(x, norm1_g, w_in, sink, w_fourier, attn_out_g, fourier_out_g, w_out, norm2_g,
           w_up, dw_w, dw_b, w_down, normf_g):
    x2 = x.reshape(SEQ, D_MODEL)
    q, kv, u_sw = _in_proj(x2, norm1_g.reshape(1, D_MODEL), w_in)
    a = _attention(sink, q, kv, attn_out_g.reshape(1, ATTN_WIDTH))
    y = _fourier(u_sw, w_fourier)
    x1, h2 = _out_proj(a, y, fourier_out_g.reshape(1, FOURIER_WIDTH), w_out,
                       x2, norm2_g.reshape(1, D_MODEL))
    out = _conv_ffn(h2, w_up, dw_w.reshape(3, D_FF), dw_b.reshape(1, D_FF), w_down, x1,
                    normf_g.reshape(1, D_MODEL))
    return out.reshape(x.shape)
```

```python
import math

import numpy as np
import jax
import jax.numpy as jnp
from jax import lax
from jax.experimental import pallas as pl
from jax.experimental.pallas import tpu as pltpu

D_MODEL = 2048
SEQ = 8192
HEAD_DIM = 128
FOURIER_WIDTH = 512
ATTN_WIDTH = 1536
N_Q_HEADS = 12
GQA_GROUP = 3
N_KV_HEADS = 4
KV_WIDTH = 512
FOURIER_GROUP_DIM = 128
N_FOURIER_GROUPS = 4
IN_WIDTH = 3072
WINDOW = 128
BLOCK = 128
D_FF = 5504
EPS = 1e-6
NEG_INF = -1e30
LOG2E = math.log2(math.e)

VMEM_LIMIT_BYTES = 60 * 1024 * 1024
SUBLANES = 8

DFT_N1 = 128
DFT_N2 = SEQ // DFT_N1

ROW_TILE = 512
ATTN_Q_BLOCKS = 2
FF_ROW_TILE = 1024
FF_CHUNK = 256
FF_ROW_BLOCK = 128
X1_PANEL = 256
N_FF_CHUNKS = -(-D_FF // FF_CHUNK)
HALO_ROWS = 16


def _alibi_slopes(n_heads):
    def pow2_slopes(n):
        start = 2.0 ** (-8.0 / n)
        return [start ** (i + 1) for i in range(n)]
    if math.log2(n_heads).is_integer():
        s = pow2_slopes(n_heads)
    else:
        closest = 2 ** int(math.floor(math.log2(n_heads)))
        s = pow2_slopes(closest) + pow2_slopes(2 * closest)[0::2][: n_heads - closest]
    return [float(np.float32(v)) for v in s]


def _rms_scale(v):
    return lax.rsqrt(jnp.mean(v * v, axis=-1, keepdims=True) + EPS)


def _params(*semantics):
    return pltpu.CompilerParams(dimension_semantics=semantics, vmem_limit_bytes=VMEM_LIMIT_BYTES)


def _in_proj_kernel(x_ref, g_ref, w_ref, q_ref, kv_ref, u_ref):
    x = x_ref[...]
    h = (x * _rms_scale(x) * g_ref[...]).astype(jnp.bfloat16)
    scale = HEAD_DIM ** -0.5 * LOG2E
    cw = 512
    for c in range(IN_WIDTH // cw):
        r = jnp.dot(h, w_ref[:, c * cw:(c + 1) * cw].astype(jnp.bfloat16),
                    preferred_element_type=jnp.float32)
        if c < 3:
            q_ref[:, c * cw:(c + 1) * cw] = (r * scale).astype(jnp.bfloat16)
        elif c < 5:
            kv_ref[:, (c - 3) * cw:(c - 2) * cw] = r.astype(jnp.bfloat16)
        else:
            n1_per_tile = ROW_TILE // DFT_N2
            u3d = u_ref.reshape(N_FOURIER_GROUPS, ROW_TILE, FOURIER_GROUP_DIM)
            for g in range(N_FOURIER_GROUPS):
                for n1l in range(n1_per_tile):
                    u3d[g, pl.ds(n1l, DFT_N2, stride=n1_per_tile), :] = r[
                        n1l * DFT_N2:(n1l + 1) * DFT_N2,
                        g * FOURIER_GROUP_DIM:(g + 1) * FOURIER_GROUP_DIM]


def _in_proj(x, g, w):
    n_tiles = SEQ // ROW_TILE
    return pl.pallas_call(
        _in_proj_kernel,
        grid=(n_tiles,),
        in_specs=[
            pl.BlockSpec((ROW_TILE, D_MODEL), lambda i: (i, 0)),
            pl.BlockSpec((1, D_MODEL), lambda i: (0, 0)),
            pl.BlockSpec((D_MODEL, IN_WIDTH), lambda i: (0, 0), pipeline_mode=pl.Buffered(1)),
        ],
        out_specs=[
            pl.BlockSpec((ROW_TILE, ATTN_WIDTH), lambda i: (i, 0)),
            pl.BlockSpec((ROW_TILE, 2 * KV_WIDTH), lambda i: (i, 0)),
            pl.BlockSpec((N_FOURIER_GROUPS, DFT_N2, ROW_TILE // DFT_N2, FOURIER_GROUP_DIM),
                         lambda i: (0, 0, i, 0)),
        ],
        out_shape=[
            jax.ShapeDtypeStruct((SEQ, ATTN_WIDTH), jnp.bfloat16),
            jax.ShapeDtypeStruct((SEQ, 2 * KV_WIDTH), jnp.bfloat16),
            jax.ShapeDtypeStruct((N_FOURIER_GROUPS, DFT_N2, DFT_N1, FOURIER_GROUP_DIM), jnp.float32),
        ],
        compiler_params=_params("arbitrary"),
        name="in_proj",
    )(x, g, w)


def _attention_kernel(sink_ref, q_ref, kvp_ref, kvc_ref, kvn_ref, g_ref, o_ref, acc_ref):
    i = pl.program_id(0)
    slopes = _alibi_slopes(N_Q_HEADS)
    qi = lax.broadcasted_iota(jnp.int32, (BLOCK, 3 * BLOCK), 0)
    kj = lax.broadcasted_iota(jnp.int32, (BLOCK, 3 * BLOCK), 1)
    arel = jnp.abs(kj - BLOCK - qi)
    in_band = arel <= WINDOW
    neg_arel = -arel.astype(jnp.float32)

    for qb in range(ATTN_Q_BLOCKS):
        rows = slice(qb * BLOCK, (qb + 1) * BLOCK)
        s_abs = (i * ATTN_Q_BLOCKS + qb) * BLOCK - BLOCK + kj
        valid = in_band & (s_abs >= 0) & (s_abs < SEQ)
        base = jnp.where(valid, neg_arel, NEG_INF)

        def band(col0, qb=qb):
            sl = slice(col0, col0 + HEAD_DIM)
            blocks = ([kvp_ref[:, sl]]
                      + [kvc_ref[b * BLOCK:(b + 1) * BLOCK, sl] for b in range(ATTN_Q_BLOCKS)]
                      + [kvn_ref[:, sl]])
            return jnp.concatenate(blocks[qb:qb + 3], axis=0)

        scores = []
        for kh in range(N_KV_HEADS):
            qg = jnp.concatenate(
                [q_ref[rows, (kh * GQA_GROUP + g) * HEAD_DIM:(kh * GQA_GROUP + g + 1) * HEAD_DIM]
                 for g in range(GQA_GROUP)], axis=0)
            scores.append(lax.dot_general(qg, band(kh * HEAD_DIM), (((1,), (1,)), ((), ())),
                                          preferred_element_type=jnp.float32))

        for kh in range(N_KV_HEADS):
            s = scores[kh]
            vcat = band(KV_WIDTH + kh * HEAD_DIM)
            ps, dens = [], []
            for g in range(GQA_GROUP):
                h = kh * GQA_GROUP + g
                sink_h = sink_ref[h] * LOG2E
                sg = s[g * BLOCK:(g + 1) * BLOCK] + (slopes[h] * LOG2E) * base
                m = jnp.maximum(jnp.max(sg, axis=-1, keepdims=True), sink_h)
                p = jnp.exp2(sg - m)
                dens.append(jnp.sum(p, axis=-1, keepdims=True) + jnp.exp2(sink_h - m))
                ps.append(p.astype(jnp.bfloat16))
            o = jnp.dot(jnp.concatenate(ps, axis=0), vcat, preferred_element_type=jnp.float32)
            for g in range(GQA_GROUP):
                h = kh * GQA_GROUP + g
                acc_ref[rows, h * HEAD_DIM:(h + 1) * HEAD_DIM] = o[g * BLOCK:(g + 1) * BLOCK] / dens[g]

    a = acc_ref[...]
    o_ref[...] = (a * _rms_scale(a) * g_ref[...]).astype(jnp.bfloat16)


def _attention(sink, q, kv, g):
    nb = SEQ // BLOCK
    Q = ATTN_Q_BLOCKS
    return pl.pallas_call(
        _attention_kernel,
        grid_spec=pltpu.PrefetchScalarGridSpec(
            num_scalar_prefetch=1,
            grid=(nb // Q,),
            in_specs=[
                pl.BlockSpec((Q * BLOCK, ATTN_WIDTH), lambda i, s: (i, 0)),
                pl.BlockSpec((BLOCK, 2 * KV_WIDTH), lambda i, s: (jnp.maximum(Q * i - 1, 0), 0)),
                pl.BlockSpec((Q * BLOCK, 2 * KV_WIDTH), lambda i, s: (i, 0)),
                pl.BlockSpec((BLOCK, 2 * KV_WIDTH),
                             lambda i, s: (jnp.minimum(Q * (i + 1), nb - 1), 0)),
                pl.BlockSpec((1, ATTN_WIDTH), lambda i, s: (0, 0)),
            ],
            out_specs=pl.BlockSpec((Q * BLOCK, ATTN_WIDTH), lambda i, s: (i, 0)),
            scratch_shapes=[pltpu.VMEM((Q * BLOCK, ATTN_WIDTH), jnp.float32)],
        ),
        out_shape=jax.ShapeDtypeStruct((SEQ, ATTN_WIDTH), jnp.bfloat16),
        compiler_params=_params("arbitrary"),
        name="attention",
    )(sink, q, kv, kv, kv, g)


def _dft_tables():
    n1 = np.arange(DFT_N1, dtype=np.float64)
    n2 = np.arange(DFT_N2, dtype=np.float64)
    ang1 = -2.0 * np.pi * (np.outer(n1, n1)[None] / DFT_N1 + (n2[:, None, None] * n1[None, :, None]) / SEQ)
    f1 = np.concatenate([np.cos(ang1), np.sin(ang1)], axis=1) / np.sqrt(DFT_N1)
    ang2 = -2.0 * np.pi * np.outer(n2, n2) / DFT_N2
    f2r, f2i = np.cos(ang2) / np.sqrt(DFT_N2), np.sin(ang2) / np.sqrt(DFT_N2)
    f2 = np.block([[f2r, -f2i], [f2i, f2r]])
    c = np.arange(FOURIER_GROUP_DIM, dtype=np.float64)
    angc = 2.0 * np.pi * np.outer(c, c) / FOURIER_GROUP_DIM
    cs = np.concatenate([np.cos(angc), np.sin(angc)], axis=0) / np.sqrt(FOURIER_GROUP_DIM)
    return (jnp.asarray(f1, dtype=jnp.float32), jnp.asarray(f2, dtype=jnp.float32),
            jnp.asarray(cs, dtype=jnp.float32))


DFT_BATCH = SUBLANES


def _group_lanes(ref, rows):
    return jnp.concatenate([ref[g, rows, :] for g in range(N_FOURIER_GROUPS)], axis=1)


def _dft_stage1_kernel(u_ref, f1_ref, yr_ref, yi_ref):
    G, C = N_FOURIER_GROUPS, FOURIER_GROUP_DIM
    yr3d = yr_ref.reshape(G, DFT_N1 * DFT_BATCH, C)
    yi3d = yi_ref.reshape(G, DFT_N1 * DFT_BATCH, C)
    for j in range(DFT_BATCH):
        d = _group_lanes(u_ref, slice(j * DFT_N1, (j + 1) * DFT_N1)).astype(jnp.bfloat16)
        y = jnp.dot(f1_ref[j].astype(jnp.bfloat16), d, preferred_element_type=jnp.float32)
        for g in range(G):
            yr3d[g, pl.ds(j, DFT_N1, stride=DFT_BATCH), :] = y[:DFT_N1, g * C:(g + 1) * C]
            yi3d[g, pl.ds(j, DFT_N1, stride=DFT_BATCH), :] = y[DFT_N1:, g * C:(g + 1) * C]


def _dft_stage1(u, f1):
    G, C = N_FOURIER_GROUPS, FOURIER_GROUP_DIM
    n_steps = DFT_N2 // DFT_BATCH
    y_shape = jax.ShapeDtypeStruct((G, DFT_N1, DFT_N2, C), jnp.float32)
    y_spec = pl.BlockSpec((G, DFT_N1, DFT_BATCH, C), lambda s: (0, 0, s, 0))
    return pl.pallas_call(
        _dft_stage1_kernel,
        grid=(n_steps,),
        in_specs=[
            pl.BlockSpec((G, DFT_BATCH * DFT_N1, C), lambda s: (0, s, 0)),
            pl.BlockSpec((DFT_BATCH, 2 * DFT_N1, DFT_N1), lambda s: (s, 0, 0)),
        ],
        out_specs=[y_spec, y_spec],
        out_shape=[y_shape, y_shape],
        compiler_params=_params("arbitrary"),
        name="dft_stage1",
    )(u, f1)


def _dft_stage2_kernel(yr_ref, yi_ref, f2_ref, cs_ref, w_ref, y_ref, x_ref, m_ref):
    G, C = N_FOURIER_GROUPS, FOURIER_GROUP_DIM

    @pl.when(pl.program_id(0) == 0)
    def _():
        for g in range(G):
            m_ref[g] = jnp.dot(cs_ref[...].astype(jnp.bfloat16), w_ref[g].astype(jnp.bfloat16),
                               preferred_element_type=jnp.float32).astype(jnp.bfloat16)

    f2 = f2_ref[...].astype(jnp.bfloat16)
    for j in range(DFT_BATCH):
        rows = slice(j * DFT_N2, (j + 1) * DFT_N2)
        d = jnp.concatenate([_group_lanes(yr_ref, rows), _group_lanes(yi_ref, rows)],
                            axis=0).astype(jnp.bfloat16)
        xc = jnp.dot(f2, d, preferred_element_type=jnp.float32)
        for g in range(G):
            x_ref[g, pl.ds(j, DFT_N2, stride=DFT_BATCH), :] = xc[:DFT_N2, g * C:(g + 1) * C]
            x_ref[G + g, pl.ds(j, DFT_N2, stride=DFT_BATCH), :] = xc[DFT_N2:, g * C:(g + 1) * C]

    for g in range(G):
        m = m_ref[g]
        y_ref[g] = (
            jnp.dot(x_ref[g].astype(jnp.bfloat16), m[:C], preferred_element_type=jnp.float32)
            + jnp.dot(x_ref[G + g].astype(jnp.bfloat16), m[C:], preferred_element_type=jnp.float32)
        ).reshape(DFT_N2, DFT_BATCH, C)


def _dft_stage2(yr, yi, f2, cs, w_fourier):
    G, C = N_FOURIER_GROUPS, FOURIER_GROUP_DIM
    n_steps = DFT_N1 // DFT_BATCH
    y_spec = pl.BlockSpec((G, DFT_BATCH * DFT_N2, C), lambda s: (0, s, 0))
    return pl.pallas_call(
        _dft_stage2_kernel,
        grid=(n_steps,),
        in_specs=[
            y_spec, y_spec,
            pl.BlockSpec((2 * DFT_N2, 2 * DFT_N2), lambda s: (0, 0)),
            pl.BlockSpec((2 * C, C), lambda s: (0, 0)),
            pl.BlockSpec((G, C, C), lambda s: (0, 0, 0)),
        ],
        out_specs=pl.BlockSpec((G, DFT_N2, DFT_BATCH, C), lambda s: (0, 0, s, 0)),
        out_shape=jax.ShapeDtypeStruct((G, DFT_N2, DFT_N1, C), jnp.float32),
        scratch_shapes=[
            pltpu.VMEM((2 * G, DFT_BATCH * DFT_N2, C), jnp.float32),
            pltpu.VMEM((G, 2 * C, C), jnp.bfloat16),
        ],
        compiler_params=_params("arbitrary"),
        name="dft_stage2",
    )(yr, yi, f2, cs, w_fourier)


def _fourier(u_sw, w_fourier):
    G, C = N_FOURIER_GROUPS, FOURIER_GROUP_DIM
    f1, f2, cs = _dft_tables()
    yr, yi = _dft_stage1(u_sw.reshape(G, SEQ, C), f1)
    y = _dft_stage2(yr.reshape(G, SEQ, C), yi.reshape(G, SEQ, C), f2, cs, w_fourier)
    return y.reshape(G, SEQ, C)


def _out_proj_kernel(a_ref, y_ref, gf_ref, w_ref, x_ref, g2_ref, x1_ref, h2_ref):
    y = _group_lanes(y_ref, slice(None))
    f = (y * _rms_scale(y) * gf_ref[...]).astype(jnp.bfloat16)
    a = a_ref[...]
    cw = 512
    ssq = jnp.zeros((ROW_TILE, 1), jnp.float32)
    for c in range(D_MODEL // cw):
        cs = slice(c * cw, (c + 1) * cw)
        r = (jnp.dot(a, w_ref[:ATTN_WIDTH, cs].astype(jnp.bfloat16),
                     preferred_element_type=jnp.float32)
             + jnp.dot(f, w_ref[ATTN_WIDTH:, cs].astype(jnp.bfloat16),
                       preferred_element_type=jnp.float32))
        x1 = x_ref[:, cs] + r
        x1_ref[:, cs] = x1
        ssq = ssq + jnp.sum(x1 * x1, axis=-1, keepdims=True)
    rs = lax.rsqrt(ssq * (1.0 / D_MODEL) + EPS)
    h2_ref[...] = (x1_ref[...] * rs * g2_ref[...]).astype(jnp.bfloat16)


def _out_proj(a, y, gf, w, x, g2):
    n_tiles = SEQ // ROW_TILE
    return pl.pallas_call(
        _out_proj_kernel,
        grid=(n_tiles,),
        in_specs=[
            pl.BlockSpec((ROW_TILE, ATTN_WIDTH), lambda i: (i, 0)),
            pl.BlockSpec((N_FOURIER_GROUPS, ROW_TILE, FOURIER_GROUP_DIM), lambda i: (0, i, 0)),
            pl.BlockSpec((1, FOURIER_WIDTH), lambda i: (0, 0)),
            pl.BlockSpec((D_MODEL, D_MODEL), lambda i: (0, 0), pipeline_mode=pl.Buffered(1)),
            pl.BlockSpec((ROW_TILE, D_MODEL), lambda i: (i, 0)),
            pl.BlockSpec((1, D_MODEL), lambda i: (0, 0)),
        ],
        out_specs=[
            pl.BlockSpec((ROW_TILE, D_MODEL), lambda i: (i, 0)),
            pl.BlockSpec((ROW_TILE, D_MODEL), lambda i: (i, 0)),
        ],
        out_shape=[
            jax.ShapeDtypeStruct((SEQ, D_MODEL), jnp.float32),
            jax.ShapeDtypeStruct((SEQ, D_MODEL), jnp.bfloat16),
        ],
        compiler_params=_params("arbitrary"),
        name="out_proj",
    )(a, y, gf, w, x, g2)


def _gelu(v):
    return 0.5 * v * (1.0 + lax.erf(v * (2.0 ** -0.5)))


def _ff_chunk_start(c):
    return pl.multiple_of(jnp.minimum(c * FF_CHUNK, D_FF - FF_CHUNK), 128)


def _ffn_step(chunk, hx_ref, wg_ref, wv_ref, up_refs, dww_ref, dwb_ref, wd_ref, down_refs, o_ref):
    H, T, B = HALO_ROWS, FF_ROW_TILE, FF_ROW_BLOCK
    n_blocks = T // B
    if up_refs is not None:
        wg = wg_ref[...].astype(jnp.bfloat16)
        wv = wv_ref[...].astype(jnp.bfloat16)
        gate_up, val_up = up_refs
        gate_edges = [(T + 2 * H) * b // n_blocks // H * H for b in range(n_blocks + 1)]
    gate_dn, val_dn = down_refs
    w = dww_ref[...]
    wd = wd_ref[...].astype(jnp.bfloat16)
    overlap = chunk * FF_CHUNK - _ff_chunk_start(chunk)
    keep = lax.broadcasted_iota(jnp.int32, (1, FF_CHUNK), 1) >= overlap
    for b in range(n_blocks):
        if up_refs is not None:
            g0, g1 = gate_edges[b], gate_edges[b + 1]
            gate_up[g0:g1] = jnp.dot(hx_ref[g0:g1], wg, preferred_element_type=jnp.float32)
            val_up[b * B:(b + 1) * B] = jnp.dot(hx_ref[H + b * B:H + (b + 1) * B], wv,
                                                preferred_element_type=jnp.float32)
        r0, r1 = b * B, (b + 1) * B
        conv = (w[0:1] * gate_dn[H + r0 - 1:H + r1 - 1] + w[1:2] * gate_dn[H + r0:H + r1]
                + w[2:3] * gate_dn[H + r0 + 1:H + r1 + 1] + dwb_ref[...])
        act = jnp.where(keep, _gelu(conv) * val_dn[r0:r1], 0.0).astype(jnp.bfloat16)
        o_ref[r0:r1] += jnp.dot(act, wd, preferred_element_type=jnp.float32)


def _conv_ffn_kernel(h_ref, hp_ref, hn_ref, wg_ref, wv_ref, dww_ref, dwb_ref, wd_ref,
                     x1_ref, gf_ref, o_ref, hx_ref, gate0_ref, val0_ref, gate1_ref, val1_ref):
    i = pl.program_id(0)
    c = pl.program_id(1)
    T = FF_ROW_TILE
    H = HALO_ROWS
    prev = jnp.maximum(c - 1, 0)

    @pl.when(c == 0)
    def _():
        hx_ref[0:H] = jnp.where(i == 0, jnp.zeros_like(hp_ref), hp_ref[...])
        hx_ref[H:H + T] = h_ref[...]
        hx_ref[H + T:] = jnp.where(i == pl.num_programs(0) - 1, jnp.zeros_like(hn_ref), hn_ref[...])
        o_ref[...] = jnp.zeros_like(o_ref)
        gate1_ref[...] = jnp.zeros_like(gate1_ref)
        val1_ref[...] = jnp.zeros_like(val1_ref)

    for k in range(D_MODEL // X1_PANEL):
        @pl.when(c == k)
        def _(k=k):
            o_ref[:, k * X1_PANEL:(k + 1) * X1_PANEL] += x1_ref[...]

    slot0, slot1 = (gate0_ref, val0_ref), (gate1_ref, val1_ref)

    @pl.when((c < N_FF_CHUNKS) & (c % 2 == 0))
    def _():
        _ffn_step(prev, hx_ref, wg_ref, wv_ref, slot0, dww_ref, dwb_ref, wd_ref, slot1, o_ref)

    @pl.when((c < N_FF_CHUNKS) & (c % 2 == 1))
    def _():
        _ffn_step(prev, hx_ref, wg_ref, wv_ref, slot1, dww_ref, dwb_ref, wd_ref, slot0, o_ref)

    @pl.when(c == N_FF_CHUNKS)
    def _():
        last = slot0 if (N_FF_CHUNKS - 1) % 2 == 0 else slot1
        _ffn_step(prev, hx_ref, wg_ref, wv_ref, None, dww_ref, dwb_ref, wd_ref, last, o_ref)
        o = o_ref[...]
        o_ref[...] = o * _rms_scale(o) * gf_ref[...]


def _conv_ffn(h2, w_up, dww, dwb, w_down, x1, gf):
    n_tiles = SEQ // FF_ROW_TILE
    halo_per_tile = FF_ROW_TILE // HALO_ROWS
    n_halo = SEQ // HALO_ROWS
    E = pl.Element
    T, H = FF_ROW_TILE, HALO_ROWS

    def up_start(c):
        return _ff_chunk_start(jnp.minimum(c, N_FF_CHUNKS - 1))

    def down_start(c):
        return _ff_chunk_start(jnp.maximum(c - 1, 0))

    return pl.pallas_call(
        _conv_ffn_kernel,
        grid=(n_tiles, N_FF_CHUNKS + 1),
        in_specs=[
            pl.BlockSpec((T, D_MODEL), lambda i, c: (i, 0)),
            pl.BlockSpec((H, D_MODEL), lambda i, c: (jnp.maximum(i * halo_per_tile - 1, 0), 0)),
            pl.BlockSpec((H, D_MODEL),
                         lambda i, c: (jnp.minimum((i + 1) * halo_per_tile, n_halo - 1), 0)),
            pl.BlockSpec((E(D_MODEL), E(FF_CHUNK)), lambda i, c: (0, up_start(c))),
            pl.BlockSpec((E(D_MODEL), E(FF_CHUNK)),
                         lambda i, c: (0, pl.multiple_of(D_FF + up_start(c), 128))),
            pl.BlockSpec((E(3), E(FF_CHUNK)), lambda i, c: (0, down_start(c))),
            pl.BlockSpec((E(1), E(FF_CHUNK)), lambda i, c: (0, down_start(c))),
            pl.BlockSpec((E(FF_CHUNK), E(D_MODEL)), lambda i, c: (down_start(c), 0)),
            pl.BlockSpec((T, X1_PANEL),
                         lambda i, c: (i, jnp.minimum(c, D_MODEL // X1_PANEL - 1))),
            pl.BlockSpec((1, D_MODEL), lambda i, c: (0, 0)),
        ],
        out_specs=pl.BlockSpec((T, D_MODEL), lambda i, c: (i, 0)),
        out_shape=jax.ShapeDtypeStruct((SEQ, D_MODEL), jnp.float32),
        scratch_shapes=[
            pltpu.VMEM((T + 2 * H, D_MODEL), jnp.bfloat16),
            pltpu.VMEM((T + 2 * H, FF_CHUNK), jnp.float32), pltpu.VMEM((T, FF_CHUNK), jnp.float32),
            pltpu.VMEM((T + 2 * H, FF_CHUNK), jnp.float32), pltpu.VMEM((T, FF_CHUNK), jnp.float32),
        ],
        compiler_params=_params("arbitrary", "arbitrary"),
        name="conv_ffn",
    )(h2, h2, h2, w_up, w_up, dww, dwb, w_down, x1, gf)


def kernel(x, norm1_g, w_in, sink, w_fourier, attn_out_g, fourier_out_g, w_out, norm2_g,
           w_up, dw_w, dw_b, w_down, normf_g):
    x2 = x.reshape(SEQ, D_MODEL)
    q, kv, u_sw = _in_proj(x2, norm1_g.reshape(1, D_MODEL), w_in)
    a = _attention(sink, q, kv, attn_out_g.reshape(1, ATTN_WIDTH))
    y = _fourier(u_sw, w_fourier)
    x1, h2 = _out_proj(a, y, fourier_out_g.reshape(1, FOURIER_WIDTH), w_out,
                       x2, norm2_g.reshape(1, D_MODEL))
    out = _conv_ffn(h2, w_up, dw_w.reshape(3, D_FF), dw_b.reshape(1, D_FF), w_down, x1,
                    normf_g.reshape(1, D_MODEL))
    return out.reshape(x.shape)
```

```python
import math

import numpy as np
import jax
import jax.numpy as jnp
from jax import lax
from jax.experimental import pallas as pl
from jax.experimental.pallas import tpu as pltpu

D_MODEL = 2048
SEQ = 8192
HEAD_DIM = 128
FOURIER_WIDTH = 512
ATTN_WIDTH = 1536
N_Q_HEADS = 12
GQA_GROUP = 3
N_KV_HEADS = 4
KV_WIDTH = 512
FOURIER_GROUP_DIM = 128
N_FOURIER_GROUPS = 4
IN_WIDTH = 3072
WINDOW = 128
BLOCK = 128
D_FF = 5504
EPS = 1e-6
NEG_INF = -1e30
LOG2E = math.log2(math.e)

VMEM_LIMIT_BYTES = 60 * 1024 * 1024
SUBLANES = 8

DFT_N1 = 128
DFT_N2 = SEQ // DFT_N1

ROW_TILE = 512
ATTN_Q_BLOCKS = 8
FF_ROW_TILE = 1024
FF_CHUNK = 256
FF_ROW_BLOCK = 128
X1_PANEL = 256
N_FF_CHUNKS = -(-D_FF // FF_CHUNK)
HALO_ROWS = 16


def _alibi_slopes(n_heads):
    def pow2_slopes(n):
        start = 2.0 ** (-8.0 / n)
        return [start ** (i + 1) for i in range(n)]
    if math.log2(n_heads).is_integer():
        s = pow2_slopes(n_heads)
    else:
        closest = 2 ** int(math.floor(math.log2(n_heads)))
        s = pow2_slopes(closest) + pow2_slopes(2 * closest)[0::2][: n_heads - closest]
    return [float(np.float32(v)) for v in s]


def _rms_scale(v):
    return lax.rsqrt(jnp.mean(v * v, axis=-1, keepdims=True) + EPS)


def _params(*semantics):
    return pltpu.CompilerParams(dimension_semantics=semantics, vmem_limit_bytes=VMEM_LIMIT_BYTES)


def _in_proj_kernel(x_ref, g_ref, w_ref, q_ref, kv_ref, u_ref):
    x = x_ref[...]
    h = (x * _rms_scale(x) * g_ref[...]).astype(jnp.bfloat16)
    scale = HEAD_DIM ** -0.5 * LOG2E
    cw = 512
    for c in range(IN_WIDTH // cw):
        r = jnp.dot(h, w_ref[:, c * cw:(c + 1) * cw].astype(jnp.bfloat16),
                    preferred_element_type=jnp.float32)
        if c < 3:
            q_ref[:, c * cw:(c + 1) * cw] = (r * scale).astype(jnp.bfloat16)
        elif c < 5:
            kv_ref[:, (c - 3) * cw:(c - 2) * cw] = r.astype(jnp.bfloat16)
        else:
            n1_per_tile = ROW_TILE // DFT_N2
            u3d = u_ref.reshape(N_FOURIER_GROUPS, ROW_TILE, FOURIER_GROUP_DIM)
            for g in range(N_FOURIER_GROUPS):
                for n1l in range(n1_per_tile):
                    u3d[g, pl.ds(n1l, DFT_N2, stride=n1_per_tile), :] = r[
                        n1l * DFT_N2:(n1l + 1) * DFT_N2,
                        g * FOURIER_GROUP_DIM:(g + 1) * FOURIER_GROUP_DIM]


def _in_proj(x, g, w):
    n_tiles = SEQ // ROW_TILE
    return pl.pallas_call(
        _in_proj_kernel,
        grid=(n_tiles,),
        in_specs=[
            pl.BlockSpec((ROW_TILE, D_MODEL), lambda i: (i, 0)),
            pl.BlockSpec((1, D_MODEL), lambda i: (0, 0)),
            pl.BlockSpec((D_MODEL, IN_WIDTH), lambda i: (0, 0), pipeline_mode=pl.Buffered(1)),
        ],
        out_specs=[
            pl.BlockSpec((ROW_TILE, ATTN_WIDTH), lambda i: (i, 0)),
            pl.BlockSpec((ROW_TILE, 2 * KV_WIDTH), lambda i: (i, 0)),
            pl.BlockSpec((N_FOURIER_GROUPS, DFT_N2, ROW_TILE // DFT_N2, FOURIER_GROUP_DIM),
                         lambda i: (0, 0, i, 0)),
        ],
        out_shape=[
            jax.ShapeDtypeStruct((SEQ, ATTN_WIDTH), jnp.bfloat16),
            jax.ShapeDtypeStruct((SEQ, 2 * KV_WIDTH), jnp.bfloat16),
            jax.ShapeDtypeStruct((N_FOURIER_GROUPS, DFT_N2, DFT_N1, FOURIER_GROUP_DIM), jnp.float32),
        ],
        compiler_params=_params("arbitrary"),
        name="in_proj",
    )(x, g, w)


def _attention_kernel(sink_ref, q_ref, kvp_ref, kvc_ref, kvn_ref, g_ref, o_ref, acc_ref):
    i = pl.program_id(0)
    slopes = _alibi_slopes(N_Q_HEADS)
    qi = lax.broadcasted_iota(jnp.int32, (BLOCK, 3 * BLOCK), 0)
    kj = lax.broadcasted_iota(jnp.int32, (BLOCK, 3 * BLOCK), 1)
    arel = jnp.abs(kj - BLOCK - qi)
    in_band = arel <= WINDOW
    neg_arel = -arel.astype(jnp.float32)

    for qb in range(ATTN_Q_BLOCKS):
        rows = slice(qb * BLOCK, (qb + 1) * BLOCK)
        s_abs = (i * ATTN_Q_BLOCKS + qb) * BLOCK - BLOCK + kj
        valid = in_band & (s_abs >= 0) & (s_abs < SEQ)
        base = jnp.where(valid, neg_arel, NEG_INF)

        def band(col0, qb=qb):
            sl = slice(col0, col0 + HEAD_DIM)
            blocks = ([kvp_ref[:, sl]]
                      + [kvc_ref[b * BLOCK:(b + 1) * BLOCK, sl] for b in range(ATTN_Q_BLOCKS)]
                      + [kvn_ref[:, sl]])
            return jnp.concatenate(blocks[qb:qb + 3], axis=0)

        scores = []
        for kh in range(N_KV_HEADS):
            qg = jnp.concatenate(
                [q_ref[rows, (kh * GQA_GROUP + g) * HEAD_DIM:(kh * GQA_GROUP + g + 1) * HEAD_DIM]
                 for g in range(GQA_GROUP)], axis=0)
            scores.append(lax.dot_general(qg, band(kh * HEAD_DIM), (((1,), (1,)), ((), ())),
                                          preferred_element_type=jnp.float32))

        for kh in range(N_KV_HEADS):
            s = scores[kh]
            vcat = band(KV_WIDTH + kh * HEAD_DIM)
            ps, dens = [], []
            for g in range(GQA_GROUP):
                h = kh * GQA_GROUP + g
                sink_h = sink_ref[h] * LOG2E
                sg = s[g * BLOCK:(g + 1) * BLOCK] + (slopes[h] * LOG2E) * base
                m = jnp.maximum(jnp.max(sg, axis=-1, keepdims=True), sink_h)
                p = jnp.exp2(sg - m)
                dens.append(jnp.sum(p, axis=-1, keepdims=True) + jnp.exp2(sink_h - m))
                ps.append(p.astype(jnp.bfloat16))
            o = jnp.dot(jnp.concatenate(ps, axis=0), vcat, preferred_element_type=jnp.float32)
            for g in range(GQA_GROUP):
                h = kh * GQA_GROUP + g
                acc_ref[rows, h * HEAD_DIM:(h + 1) * HEAD_DIM] = o[g * BLOCK:(g + 1) * BLOCK] / dens[g]

    a = acc_ref[...]
    o_ref[...] = (a * _rms_scale(a) * g_ref[...]).astype(jnp.bfloat16)


def _attention(sink, q, kv, g):
    nb = SEQ // BLOCK
    Q = ATTN_Q_BLOCKS
    return pl.pallas_call(
        _attention_kernel,
        grid_spec=pltpu.PrefetchScalarGridSpec(
            num_scalar_prefetch=1,
            grid=(nb // Q,),
            in_specs=[
                pl.BlockSpec((Q * BLOCK, ATTN_WIDTH), lambda i, s: (i, 0)),
                pl.BlockSpec((BLOCK, 2 * KV_WIDTH), lambda i, s: (jnp.maximum(Q * i - 1, 0), 0)),
                pl.BlockSpec((Q * BLOCK, 2 * KV_WIDTH), lambda i, s: (i, 0)),
                pl.BlockSpec((BLOCK, 2 * KV_WIDTH),
                             lambda i, s: (jnp.minimum(Q * (i + 1), nb - 1), 0)),
                pl.BlockSpec((1, ATTN_WIDTH), lambda i, s: (0, 0)),
            ],
            out_specs=pl.BlockSpec((Q * BLOCK, ATTN_WIDTH), lambda i, s: (i, 0)),
            scratch_shapes=[pltpu.VMEM((Q * BLOCK, ATTN_WIDTH), jnp.float32)],
        ),
        out_shape=jax.ShapeDtypeStruct((SEQ, ATTN_WIDTH), jnp.bfloat16),
        compiler_params=_params("arbitrary"),
        name="attention",
    )(sink, q, kv, kv, kv, g)


def _dft_tables():
    n1 = np.arange(DFT_N1, dtype=np.float64)
    n2 = np.arange(DFT_N2, dtype=np.float64)
    ang1 = -2.0 * np.pi * (np.outer(n1, n1)[None] / DFT_N1 + (n2[:, None, None] * n1[None, :, None]) / SEQ)
    f1 = np.concatenate([np.cos(ang1), np.sin(ang1)], axis=1) / np.sqrt(DFT_N1)
    ang2 = -2.0 * np.pi * np.outer(n2, n2) / DFT_N2
    f2r, f2i = np.cos(ang2) / np.sqrt(DFT_N2), np.sin(ang2) / np.sqrt(DFT_N2)
    f2 = np.block([[f2r, -f2i], [f2i, f2r]])
    c = np.arange(FOURIER_GROUP_DIM, dtype=np.float64)
    angc = 2.0 * np.pi * np.outer(c, c) / FOURIER_GROUP_DIM
    cs = np.concatenate([np.cos(angc), np.sin(angc)], axis=0) / np.sqrt(FOURIER_GROUP_DIM)
    return (jnp.asarray(f1, dtype=jnp.float32), jnp.asarray(f2, dtype=jnp.float32),
            jnp.asarray(cs, dtype=jnp.float32))


DFT_BATCH = SUBLANES


def _group_lanes(ref, rows):
    return jnp.concatenate([ref[g, rows, :] for g in range(N_FOURIER_GROUPS)], axis=1)


def _dft_stage1_kernel(u_ref, f1_ref, yr_ref, yi_ref):
    G, C = N_FOURIER_GROUPS, FOURIER_GROUP_DIM
    yr3d = yr_ref.reshape(G, DFT_N1 * DFT_BATCH, C)
    yi3d = yi_ref.reshape(G, DFT_N1 * DFT_BATCH, C)
    for j in range(DFT_BATCH):
        d = _group_lanes(u_ref, slice(j * DFT_N1, (j + 1) * DFT_N1)).astype(jnp.bfloat16)
        y = jnp.dot(f1_ref[j].astype(jnp.bfloat16), d, preferred_element_type=jnp.float32)
        for g in range(G):
            yr3d[g, pl.ds(j, DFT_N1, stride=DFT_BATCH), :] = y[:DFT_N1, g * C:(g + 1) * C]
            yi3d[g, pl.ds(j, DFT_N1, stride=DFT_BATCH), :] = y[DFT_N1:, g * C:(g + 1) * C]


def _dft_stage1(u, f1):
    G, C = N_FOURIER_GROUPS, FOURIER_GROUP_DIM
    n_steps = DFT_N2 // DFT_BATCH
    y_shape = jax.ShapeDtypeStruct((G, DFT_N1, DFT_N2, C), jnp.float32)
    y_spec = pl.BlockSpec((G, DFT_N1, DFT_BATCH, C), lambda s: (0, 0, s, 0))
    return pl.pallas_call(
        _dft_stage1_kernel,
        grid=(n_steps,),
        in_specs=[
            pl.BlockSpec((G, DFT_BATCH * DFT_N1, C), lambda s: (0, s, 0)),
            pl.BlockSpec((DFT_BATCH, 2 * DFT_N1, DFT_N1), lambda s: (s, 0, 0)),
        ],
        out_specs=[y_spec, y_spec],
        out_shape=[y_shape, y_shape],
        compiler_params=_params("arbitrary"),
        name="dft_stage1",
    )(u, f1)


def _dft_stage2_kernel(yr_ref, yi_ref, f2_ref, cs_ref, w_ref, y_ref, x_ref, m_ref):
    G, C = N_FOURIER_GROUPS, FOURIER_GROUP_DIM

    @pl.when(pl.program_id(0) == 0)
    def _():
        for g in range(G):
            m_ref[g] = jnp.dot(cs_ref[...].astype(jnp.bfloat16), w_ref[g].astype(jnp.bfloat16),
                               preferred_element_type=jnp.float32).astype(jnp.bfloat16)

    f2 = f2_ref[...].astype(jnp.bfloat16)
    for j in range(DFT_BATCH):
        rows = slice(j * DFT_N2, (j + 1) * DFT_N2)
        d = jnp.concatenate([_group_lanes(yr_ref, rows), _group_lanes(yi_ref, rows)],
                            axis=0).astype(jnp.bfloat16)
        xc = jnp.dot(f2, d, preferred_element_type=jnp.float32)
        for g in range(G):
            x_ref[g, pl.ds(j, DFT_N2, stride=DFT_BATCH), :] = xc[:DFT_N2, g * C:(g + 1) * C]
            x_ref[G + g, pl.ds(j, DFT_N2, stride=DFT_BATCH), :] = xc[DFT_N2:, g * C:(g + 1) * C]

    for g in range(G):
        m = m_ref[g]
        y_ref[g] = (
            jnp.dot(x_ref[g].astype(jnp.bfloat16), m[:C], preferred_element_type=jnp.float32)
            + jnp.dot(x_ref[G + g].astype(jnp.bfloat16), m[C:], preferred_element_type=jnp.float32)
        ).reshape(DFT_N2, DFT_BATCH, C)


def _dft_stage2(yr, yi, f2, cs, w_fourier):
    G, C = N_FOURIER_GROUPS, FOURIER_GROUP_DIM
    n_steps = DFT_N1 // DFT_BATCH
    y_spec = pl.BlockSpec((G, DFT_BATCH * DFT_N2, C), lambda s: (0, s, 0))
    return pl.pallas_call(
        _dft_stage2_kernel,
        grid=(n_steps,),
        in_specs=[
            y_spec, y_spec,
            pl.BlockSpec((2 * DFT_N2, 2 * DFT_N2), lambda s: (0, 0)),
            pl.BlockSpec((2 * C, C), lambda s: (0, 0)),
            pl.BlockSpec((G, C, C), lambda s: (0, 0, 0)),
        ],
        out_specs=pl.BlockSpec((G, DFT_N2, DFT_BATCH, C), lambda s: (0, 0, s, 0)),
        out_shape=jax.ShapeDtypeStruct((G, DFT_N2, DFT_N1, C), jnp.float32),
        scratch_shapes=[
            pltpu.VMEM((2 * G, DFT_BATCH * DFT_N2, C), jnp.float32),
            pltpu.VMEM((G, 2 * C, C), jnp.bfloat16),
        ],
        compiler_params=_params("arbitrary"),
        name="dft_stage2",
    )(yr, yi, f2, cs, w_fourier)


def _fourier(u_sw, w_fourier):
    G, C = N_FOURIER_GROUPS, FOURIER_GROUP_DIM
    f1, f2, cs = _dft_tables()
    yr, yi = _dft_stage1(u_sw.reshape(G, SEQ, C), f1)
    y = _dft_stage2(yr.reshape(G, SEQ, C), yi.reshape(G, SEQ, C), f2, cs, w_fourier)
    return y.reshape(G, SEQ, C)


def _out_proj_kernel(a_ref, y_ref, gf_ref, w_ref, x_ref, g2_ref, x1_ref, h2_ref):
    y = _group_lanes(y_ref, slice(None))
    f = (y * _rms_scale(y) * gf_ref[...]).astype(jnp.bfloat16)
    af = jnp.concatenate([a_ref[...], f], axis=1)
    cw = 512
    ssq = jnp.zeros((ROW_TILE, 1), jnp.float32)
    for c in range(D_MODEL // cw):
        cs = slice(c * cw, (c + 1) * cw)
        r = jnp.dot(af, w_ref[:, cs].astype(jnp.bfloat16), preferred_element_type=jnp.float32)
        x1 = x_ref[:, cs] + r
        x1_ref[:, cs] = x1
        ssq = ssq + jnp.sum(x1 * x1, axis=-1, keepdims=True)
    rs = lax.rsqrt(ssq * (1.0 / D_MODEL) + EPS)
    h2_ref[...] = (x1_ref[...] * rs * g2_ref[...]).astype(jnp.bfloat16)


def _out_proj(a, y, gf, w, x, g2):
    n_tiles = SEQ // ROW_TILE
    return pl.pallas_call(
        _out_proj_kernel,
        grid=(n_tiles,),
        in_specs=[
            pl.BlockSpec((ROW_TILE, ATTN_WIDTH), lambda i: (i, 0)),
            pl.BlockSpec((N_FOURIER_GROUPS, ROW_TILE, FOURIER_GROUP_DIM), lambda i: (0, i, 0)),
            pl.BlockSpec((1, FOURIER_WIDTH), lambda i: (0, 0)),
            pl.BlockSpec((D_MODEL, D_MODEL), lambda i: (0, 0), pipeline_mode=pl.Buffered(1)),
            pl.BlockSpec((ROW_TILE, D_MODEL), lambda i: (i, 0)),
            pl.BlockSpec((1, D_MODEL), lambda i: (0, 0)),
        ],
        out_specs=[
            pl.BlockSpec((ROW_TILE, D_MODEL), lambda i: (i, 0)),
            pl.BlockSpec((ROW_TILE, D_MODEL), lambda i: (i, 0)),
        ],
        out_shape=[
            jax.ShapeDtypeStruct((SEQ, D_MODEL), jnp.float32),
            jax.ShapeDtypeStruct((SEQ, D_MODEL), jnp.bfloat16),
        ],
        compiler_params=_params("arbitrary"),
        name="out_proj",
    )(a, y, gf, w, x, g2)


def _gelu(v):
    return 0.5 * v * (1.0 + lax.erf(v * (2.0 ** -0.5)))


def _ff_chunk_start(c):
    return pl.multiple_of(jnp.minimum(c * FF_CHUNK, D_FF - FF_CHUNK), 128)


def _ffn_step(chunk, hx_ref, wg_ref, wv_ref, up_refs, dww_ref, dwb_ref, wd_ref, down_refs, o_ref):
    H, T, B = HALO_ROWS, FF_ROW_TILE, FF_ROW_BLOCK
    n_blocks = T // B
    if up_refs is not None:
        wg = wg_ref[...].astype(jnp.bfloat16)
        wv = wv_ref[...].astype(jnp.bfloat16)
        gate_up, val_up = up_refs
        gate_edges = [(T + 2 * H) * b // n_blocks // H * H for b in range(n_blocks + 1)]
    gate_dn, val_dn = down_refs
    w = dww_ref[...]
    wd = wd_ref[...].astype(jnp.bfloat16)
    overlap = chunk * FF_CHUNK - _ff_chunk_start(chunk)
    keep = lax.broadcasted_iota(jnp.int32, (1, FF_CHUNK), 1) >= overlap
    for b in range(n_blocks):
        if up_refs is not None:
            g0, g1 = gate_edges[b], gate_edges[b + 1]
            gate_up[g0:g1] = jnp.dot(hx_ref[g0:g1], wg, preferred_element_type=jnp.float32)
            val_up[b * B:(b + 1) * B] = jnp.dot(hx_ref[H + b * B:H + (b + 1) * B], wv,
                                                preferred_element_type=jnp.float32)
        r0, r1 = b * B, (b + 1) * B
        conv = (w[0:1] * gate_dn[H + r0 - 1:H + r1 - 1] + w[1:2] * gate_dn[H + r0:H + r1]
                + w[2:3] * gate_dn[H + r0 + 1:H + r1 + 1] + dwb_ref[...])
        act = jnp.where(keep, _gelu(conv) * val_dn[r0:r1], 0.0).astype(jnp.bfloat16)
        o_ref[r0:r1] += jnp.dot(act, wd, preferred_element_type=jnp.float32)


def _conv_ffn_kernel(h_ref, hp_ref, hn_ref, wg_ref, wv_ref, dww_ref, dwb_ref, wd_ref,
                     x1_ref, gf_ref, o_ref, hx_ref, gate0_ref, val0_ref, gate1_ref, val1_ref):
    i = pl.program_id(0)
    c = pl.program_id(1)
    T = FF_ROW_TILE
    H = HALO_ROWS
    prev = jnp.maximum(c - 1, 0)

    @pl.when(c == 0)
    def _():
        hx_ref[0:H] = jnp.where(i == 0, jnp.zeros_like(hp_ref), hp_ref[...])
        hx_ref[H:H + T] = h_ref[...]
        hx_ref[H + T:] = jnp.where(i == pl.num_programs(0) - 1, jnp.zeros_like(hn_ref), hn_ref[...])
        o_ref[...] = jnp.zeros_like(o_ref)
        gate1_ref[...] = jnp.zeros_like(gate1_ref)
        val1_ref[...] = jnp.zeros_like(val1_ref)

    for k in range(D_MODEL // X1_PANEL):
        @pl.when(c == k)
        def _(k=k):
            o_ref[:, k * X1_PANEL:(k + 1) * X1_PANEL] += x1_ref[...]

    slot0, slot1 = (gate0_ref, val0_ref), (gate1_ref, val1_ref)

    @pl.when((c < N_FF_CHUNKS) & (c % 2 == 0))
    def _():
        _ffn_step(prev, hx_ref, wg_ref, wv_ref, slot0, dww_ref, dwb_ref, wd_ref, slot1, o_ref)

    @pl.when((c < N_FF_CHUNKS) & (c % 2 == 1))
    def _():
        _ffn_step(prev, hx_ref, wg_ref, wv_ref, slot1, dww_ref, dwb_ref, wd_ref, slot0, o_ref)

    @pl.when(c == N_FF_CHUNKS)
    def _():
        last = slot0 if (N_FF_CHUNKS - 1) % 2 == 0 else slot1
        _ffn_step(prev, hx_ref, wg_ref, wv_ref, None, dww_ref, dwb_ref, wd_ref, last, o_ref)
        o = o_ref[...]
        o_ref[...] = o * _rms_scale(o) * gf_ref[...]


def _conv_ffn(h2, w_up, dww, dwb, w_down, x1, gf):
    n_tiles = SEQ // FF_ROW_TILE
    halo_per_tile = FF_ROW_TILE // HALO_ROWS
    n_halo = SEQ // HALO_ROWS
    E = pl.Element
    T, H = FF_ROW_TILE, HALO_ROWS

    def up_start(c):
        return _ff_chunk_start(jnp.minimum(c, N_FF_CHUNKS - 1))

    def down_start(c):
        return _ff_chunk_start(jnp.maximum(c - 1, 0))

    return pl.pallas_call(
        _conv_ffn_kernel,
        grid=(n_tiles, N_FF_CHUNKS + 1),
        in_specs=[
            pl.BlockSpec((T, D_MODEL), lambda i, c: (i, 0)),
            pl.BlockSpec((H, D_MODEL), lambda i, c: (jnp.maximum(i * halo_per_tile - 1, 0), 0)),
            pl.BlockSpec((H, D_MODEL),
                         lambda i, c: (jnp.minimum((i + 1) * halo_per_tile, n_halo - 1), 0)),
            pl.BlockSpec((E(D_MODEL), E(FF_CHUNK)), lambda i, c: (0, up_start(c))),
            pl.BlockSpec((E(D_MODEL), E(FF_CHUNK)),
                         lambda i, c: (0, pl.multiple_of(D_FF + up_start(c), 128))),
            pl.BlockSpec((E(3), E(FF_CHUNK)), lambda i, c: (0, down_start(c))),
            pl.BlockSpec((E(1), E(FF_CHUNK)), lambda i, c: (0, down_start(c))),
            pl.BlockSpec((E(FF_CHUNK), E(D_MODEL)), lambda i, c: (down_start(c), 0)),
            pl.BlockSpec((T, X1_PANEL),
                         lambda i, c: (i, jnp.minimum(c, D_MODEL // X1_PANEL - 1))),
            pl.BlockSpec((1, D_MODEL), lambda i, c: (0, 0)),
        ],
        out_specs=pl.BlockSpec((T, D_MODEL), lambda i, c: (i, 0)),
        out_shape=jax.ShapeDtypeStruct((SEQ, D_MODEL), jnp.float32),
        scratch_shapes=[
            pltpu.VMEM((T + 2 * H, D_MODEL), jnp.bfloat16),
            pltpu.VMEM((T + 2 * H, FF_CHUNK), jnp.float32), pltpu.VMEM((T, FF_CHUNK), jnp.float32),
            pltpu.VMEM((T + 2 * H, FF_CHUNK), jnp.float32), pltpu.VMEM((T, FF_CHUNK), jnp.float32),
        ],
        compiler_params=_params("arbitrary", "arbitrary"),
        name="conv_ffn",
    )(h2, h2, h2, w_up, w_up, dww, dwb, w_down, x1, gf)


def kernel(x, norm1_g, w_in, sink, w_fourier, attn_out_g, fourier_out_g, w_out, norm2_g,
           w_up, dw_w, dw_b, w_down, normf_g):
    x2 = x.reshape(SEQ, D_MODEL)
    q, kv, u_sw = _in_proj(x2, norm1_g.reshape(1, D_MODEL), w_in)
    a = _attention(sink, q, kv, attn_out_g.reshape(1, ATTN_WIDTH))
    y = _fourier(u_sw, w_fourier)
    x1, h2 = _out_proj(a, y, fourier_out_g.reshape(1, FOURIER_WIDTH), w_out,
                       x2, norm2_g.reshape(1, D_MODEL))
    out = _conv_ffn(h2, w_up, dw_w.reshape(3, D_FF), dw_b.reshape(1, D_FF), w_down, x1,
                    normf_g.reshape(1, D_MODEL))
    return out.reshape(x.shape)
```
